```python
import jax, jax.numpy as jnp
from jax import lax
import numpy as np

D_MODEL = 1024
BATCH = 8
SEQ = 2048
DEPTH = 1
DEC_BATCH = 128
DEC_SEQ = 4
PAST_LEN = 16384
PAGE_SIZE = 128

D_MIX = D_MODEL
GLA_WIDTH = D_MIX // 2
GLA_HEADS = 4
GLA_DK = GLA_WIDTH // 2 // GLA_HEADS
GLA_DV = GLA_WIDTH // GLA_HEADS
GLA_RANK = 16
GLA_TAU = 16.0
GLA_CHUNK = 64
SGU_WIDTH = D_MIX - GLA_WIDTH
SGU_HEADS = 4
SGU_DH = SGU_WIDTH // SGU_HEADS
SGU_CHUNK = 128
N_MEM = 256
MEM_HEADS = 4
MEM_DH = D_MODEL // MEM_HEADS
D_FF = 2816
CONV_W = 3
EPS = 1e-6

QK_COLS = GLA_HEADS * GLA_DK
SPLIT_SIZES = (QK_COLS, QK_COLS, GLA_WIDTH, GLA_WIDTH, GLA_RANK, SGU_WIDTH, SGU_WIDTH)
D_IN = 2 * QK_COLS + 2 * GLA_WIDTH + GLA_RANK + 2 * SGU_WIDTH

kernel_name = "hymba_gla_sgu_mem_convffn_step"


def rmsnorm(x, g):
    xf = x.astype(jnp.float32)
    y = xf * lax.rsqrt(jnp.mean(xf * xf, axis=-1, keepdims=True) + EPS)
    return (y * g.astype(jnp.float32)).astype(x.dtype)


def split_cols(p):
    out, start = [], 0
    for s in SPLIT_SIZES:
        out.append(p[..., start:start + s])
        start += s
    return out


def gla_chunked(q, k, v, logg, s0):
    B, T, H, DK = q.shape
    DV = v.shape[-1]
    C = min(GLA_CHUNK, T)
    n = T // C
    f32 = jnp.float32

    def chunks(a):
        return jnp.moveaxis(a.astype(f32).reshape((B, n, C) + a.shape[2:]), 1, 0)

    causal = jnp.tril(jnp.ones((C, C), dtype=bool))[None, :, :, None, None]

    def step(S, inp):
        qc, kc, vc, gc = inp
        b = jnp.cumsum(gc, axis=1)
        decay = jnp.exp(jnp.where(causal, b[:, :, None] - b[:, None, :], -jnp.inf))
        scores = jnp.einsum('bihd,bjhd,bijhd->bijh', qc, kc, decay)
        o = (jnp.einsum('bijh,bjhv->bihv', scores, vc)
             + jnp.einsum('bihd,bhdv->bihv', qc * jnp.exp(b), S))
        b_last = b[:, -1]
        S = (jnp.exp(b_last)[..., None] * S
             + jnp.einsum('bjhd,bjhv->bhdv', kc * jnp.exp(b_last[:, None] - b), vc))
        return S, o

    S, o = lax.scan(step, s0.astype(f32), (chunks(q), chunks(k), chunks(v), chunks(logg)))
    o = jnp.moveaxis(o, 0, 1).reshape(B, T, H, DV)
    return o, S


def sgu_spatial(v, w_s, b_s):
    B, T, H, Dh = v.shape
    C = min(SGU_CHUNK, T)
    n = T // C
    w = jnp.tril(w_s[:, :C, :C])
    vc = v.reshape(B, n, C, H, Dh)
    z = jnp.einsum('hij,bnjhd->bnihd', w, vc) + b_s[:, :C].T[None, None, :, :, None]
    return z.reshape(B, T, H, Dh)


def mixer(xn, s0, lp):
    (g_mix, w_in, w_alpha, b_alpha, g_gla_out, g_sgu, w_s, b_s, w_out) = lp
    B, T, _ = xn.shape
    q, k, v, r, a, u, sv = split_cols(xn @ w_in)
    q = q.reshape(B, T, GLA_HEADS, GLA_DK) * (GLA_DK ** -0.5)
    k = k.reshape(B, T, GLA_HEADS, GLA_DK)
    v = v.reshape(B, T, GLA_HEADS, GLA_DV)
    logg = jax.nn.log_sigmoid((a @ w_alpha + b_alpha).astype(jnp.float32)) / GLA_TAU
    logg = logg.reshape(B, T, GLA_HEADS, GLA_DK)
    o, S = gla_chunked(q, k, v, logg, s0)
    o = rmsnorm(o, g_gla_out).astype(xn.dtype)
    o = o.reshape(B, T, GLA_WIDTH) * jax.nn.silu(r)
    u = jax.nn.gelu(u)
    sv = rmsnorm(jax.nn.gelu(sv).reshape(B, T, SGU_HEADS, SGU_DH), g_sgu)
    z = sgu_spatial(sv, w_s, b_s)
    s_out = u * z.reshape(B, T, SGU_WIDTH)
    y = jnp.concatenate([o, s_out], axis=-1) @ w_out
    return y, S.astype(s0.dtype), sv


def mem_kv(mem, g_mem, wk_x, wv_x):
    B = mem.shape[0]
    mn = rmsnorm(mem, g_mem)
    mk = (mn @ wk_x).reshape(B, N_MEM, MEM_HEADS, MEM_DH)
    mv = (mn @ wv_x).reshape(B, N_MEM, MEM_HEADS, MEM_DH)
    return mk, mv


def mem_attend(hn, mk, mv, wq_x, wo_x):
    B, T, _ = hn.shape
    q = (hn @ wq_x).reshape(B, T, MEM_HEADS, MEM_DH)
    s = jnp.einsum('bthd,bmhd->bhtm', q, mk).astype(jnp.float32) * (MEM_DH ** -0.5)
    p = jax.nn.softmax(s, axis=-1).astype(mv.dtype)
    o = jnp.einsum('bhtm,bmhd->bthd', p, mv).reshape(B, T, D_MODEL)
    return o @ wo_x


def conv_ffn(hn, buf, w_gate, w_up, conv_w, conv_b, w_down):
    T = hn.shape[1]
    g = hn @ w_gate
    gp = jnp.concatenate([buf.astype(g.dtype), g], axis=1)
    c = conv_b + sum(conv_w[j] * gp[:, j:j + T] for j in range(CONV_W))
    out = (jax.nn.gelu(c) * (hn @ w_up)) @ w_down
    return out, gp[:, -(CONV_W - 1):]


def layer(x, mk, mv, s0, buf, mix_p, g_x, wq_x, wo_x, g_ffn, w_gate, w_up, conv_w, conv_b, w_down):
    m, S, sv = mixer(rmsnorm(x, mix_p[0]), s0, mix_p)
    h = x + m
    h = h + mem_attend(rmsnorm(h, g_x), mk, mv, wq_x, wo_x)
    f, new_buf = conv_ffn(rmsnorm(h, g_ffn), buf, w_gate, w_up, conv_w, conv_b, w_down)
    return h + f, S, new_buf, sv


def setup_inputs(seed: int = 0) -> dict:
    key = jax.random.key(seed)
    ks = iter(jax.random.split(key, 40))
    nrm = lambda shape, s=1.0: jax.random.normal(next(ks), shape, jnp.float32) * s
    L = DEPTH
    return {
        "x_prompt": nrm((BATCH, SEQ, D_MODEL)),
        "x_sample": nrm((DEC_BATCH, DEC_SEQ, D_MODEL)),
        "mem_prompt": nrm((BATCH, N_MEM, D_MODEL)),
        "state_gla": nrm((L, DEC_BATCH, GLA_HEADS, GLA_DK, GLA_DV), 0.5),
        "state_conv": nrm((L, DEC_BATCH, CONV_W - 1, D_FF)),
        "cache_mem_k": nrm((L, DEC_BATCH, N_MEM, MEM_HEADS, MEM_DH)),
        "cache_mem_v": nrm((L, DEC_BATCH, N_MEM, MEM_HEADS, MEM_DH)),
        "g_mix": 1.0 + nrm((L, D_MODEL), 0.02),
        "w_in": nrm((L, D_MODEL, D_IN), D_MODEL ** -0.5),
        "w_alpha": nrm((L, GLA_RANK, QK_COLS), GLA_RANK ** -0.5),
        "b_alpha": nrm((L, QK_COLS), 0.02),
        "g_gla_out": 1.0 + nrm((L, GLA_HEADS, GLA_DV), 0.02),
        "g_sgu": 1.0 + nrm((L, SGU_HEADS, SGU_DH), 0.02),
        "w_s": nrm((L, SGU_HEADS, SGU_CHUNK, SGU_CHUNK), 0.5 * SGU_CHUNK ** -0.5),
        "b_s": 1.0 + nrm((L, SGU_HEADS, SGU_CHUNK), 0.02),
        "w_out": nrm((L, D_MIX, D_MODEL), D_MIX ** -0.5),
        "g_x": 1.0 + nrm((L, D_MODEL), 0.02),
        "g_mem": 1.0 + nrm((L, D_MODEL), 0.02),
        "wq_x": nrm((L, D_MODEL, D_MODEL), D_MODEL ** -0.5),
        "wk_x": nrm((L, D_MODEL, D_MODEL), D_MODEL ** -0.5),
        "wv_x": nrm((L, D_MODEL, D_MODEL), D_MODEL ** -0.5),
        "wo_x": nrm((L, D_MODEL, D_MODEL), D_MODEL ** -0.5),
        "g_ffn": 1.0 + nrm((L, D_MODEL), 0.02),
        "w_gate": nrm((L, D_MODEL, D_FF), D_MODEL ** -0.5),
        "w_up": nrm((L, D_MODEL, D_FF), D_MODEL ** -0.5),
        "conv_w": nrm((L, CONV_W, D_FF), CONV_W ** -0.5),
        "conv_b": nrm((L, D_FF), 0.02),
        "w_down": nrm((L, D_FF, D_MODEL), D_FF ** -0.5),
        "g_final": 1.0 + nrm((D_MODEL,), 0.02),
    }


def reference(x_prompt, x_sample, mem_prompt, state_gla, state_conv, cache_mem_k, cache_mem_v,
              g_mix, w_in, w_alpha, b_alpha, g_gla_out, g_sgu, w_s, b_s, w_out,
              g_x, g_mem, wq_x, wk_x, wv_x, wo_x,
              g_ffn, w_gate, w_up, conv_w, conv_b, w_down, g_final):
    hp, hs = x_prompt, x_sample
    sg_p, sc_p, mk_p, mv_p, sg_s, sc_s, sv_s = [], [], [], [], [], [], []
    for l in range(DEPTH):
        mix_p = (g_mix[l], w_in[l], w_alpha[l], b_alpha[l], g_gla_out[l], g_sgu[l], w_s[l], b_s[l], w_out[l])
        rest = (g_x[l], wq_x[l], wo_x[l], g_ffn[l], w_gate[l], w_up[l], conv_w[l], conv_b[l], w_down[l])
        mk, mv = mem_kv(mem_prompt, g_mem[l], wk_x[l], wv_x[l])
        s0 = jnp.zeros((hp.shape[0], GLA_HEADS, GLA_DK, GLA_DV), hp.dtype)
        b0 = jnp.zeros((hp.shape[0], CONV_W - 1, D_FF), hp.dtype)
        hp, S_p, buf_p, _ = layer(hp, mk, mv, s0, b0, mix_p, *rest)
        sg_p.append(S_p); sc_p.append(buf_p); mk_p.append(mk); mv_p.append(mv)
        hs, S_s, buf_s, sv = layer(hs, cache_mem_k[l], cache_mem_v[l], state_gla[l], state_conv[l],
                                   mix_p, *rest)
        sg_s.append(S_s); sc_s.append(buf_s); sv_s.append(sv)
    y_prompt = rmsnorm(hp, g_final)
    y_sample = rmsnorm(hs, g_final)
    return (y_prompt, y_sample,
            jnp.stack(sg_p), jnp.stack(sc_p), jnp.stack(mk_p), jnp.stack(mv_p),
            jnp.stack(sg_s), jnp.stack(sc_s), jnp.stack(sv_s))
```

```python
import functools

import jax
import jax.numpy as jnp
from jax import lax
from jax.experimental import pallas as pl
from jax.experimental.pallas import tpu as pltpu

F32 = jnp.float32
BF16 = jnp.bfloat16

D_MODEL = 1024
GLA_HEADS = 4
GLA_DK = 64
GLA_DV = 128
QK_COLS = GLA_HEADS * GLA_DK
GLA_WIDTH = GLA_HEADS * GLA_DV
GLA_RANK = 16
GLA_TAU = 16.0
SGU_HEADS = 4
SGU_DH = 128
SGU_WIDTH = SGU_HEADS * SGU_DH
SGU_CHUNK = 128
N_MEM = 256
MEM_HEADS = 4
MEM_DH = 256
D_FF = 2816
CONV_W = 3
EPS = 1e-6

LANES = 128
RANK_PAD = LANES
D_IN_PAD = 2 * QK_COLS + 2 * GLA_WIDTH + 2 * SGU_WIDTH + RANK_PAD
TT = 256
GLA_BLOCK = 64
SROWS = 32
STOK = SROWS * 4
FF_CHUNK = 256
ATT_ROWS = 8
VMEM_LIMIT = 56 * 1024 * 1024


def _dot(a, b):
    return jnp.dot(a.astype(BF16), b.astype(BF16), preferred_element_type=F32)


def _dot_nt(a, b):
    return lax.dot_general(a.astype(BF16), b.astype(BF16), (((1,), (1,)), ((), ())),
                           preferred_element_type=F32)


def _dot_tn(a, b):
    return lax.dot_general(a.astype(BF16), b.astype(BF16), (((0,), (0,)), ((), ())),
                           preferred_element_type=F32)


def _split3(x):
    hi = x.astype(BF16)
    r1 = x - hi.astype(F32)
    mid = r1.astype(BF16)
    lo = (r1 - mid.astype(F32)).astype(BF16)
    return hi, mid, lo


def _dot_exact_lhs(sel, parts):
    hi, mid, lo = parts
    return (jnp.dot(sel, hi, preferred_element_type=F32)
            + jnp.dot(sel, mid, preferred_element_type=F32)
            + jnp.dot(sel, lo, preferred_element_type=F32))


def _rms(x, g):
    ms = jnp.mean(x * x, axis=-1, keepdims=True)
    return x * lax.rsqrt(ms + EPS) * g


def _rms_heads(x, g, nh, dh):
    outs = []
    for h in range(nh):
        xh = x[:, h * dh:(h + 1) * dh]
        ms = jnp.mean(xh * xh, axis=-1, keepdims=True)
        outs.append(xh * lax.rsqrt(ms + EPS))
    return jnp.concatenate(outs, axis=-1) * g


def _gelu(x):
    c = 0.7978845608028654
    return x * (0.5 * (1.0 + jnp.tanh(c * (x + 0.044715 * (x * x * x)))))


def _silu(x):
    return x * (1.0 / (1.0 + jnp.exp(-x)))


def _log_sigmoid(x):
    return jnp.minimum(x, 0.0) - jnp.log1p(jnp.exp(-jnp.abs(x)))


def _iota(shape, dim):
    return lax.broadcasted_iota(jnp.int32, shape, dim)


def _in_proj(x, gmix, w_in, w_alpha, b_alpha):
    xn = _rms(x, gmix)
    p = _dot(xn, w_in)
    q = p[:, 0:256] * (GLA_DK ** -0.5)
    k = p[:, 256:512]
    v = p[:, 512:1024]
    r = p[:, 1024:1536]
    u = p[:, 1536:2048]
    sv = p[:, 2048:2560]
    a = p[:, 2560:2688]
    xg = _dot(a, w_alpha) + b_alpha
    logg = _log_sigmoid(xg) * (1.0 / GLA_TAU)
    return q, k, v, r, u, sv, logg


def _mix_out(x, o, r, u, z, ggla, w_out):
    og = _rms_heads(o, ggla, GLA_HEADS, GLA_DV) * _silu(r)
    s_out = u * z
    y = _dot(og, w_out[0:GLA_WIDTH, :]) + _dot(s_out, w_out[GLA_WIDTH:, :])
    return x + y


def _gla_prompt_tile(q, k, v, logg, st_ref):
    n = TT
    ri = _iota((n, n), 0)
    ci = _iota((n, n), 1)
    low = (ci <= ri).astype(BF16)
    bt = _dot_exact_lhs(low, _split3(logg))
    headm = (ri >> 6) == (ci >> 6)
    outs = []
    for blk in range(n // GLA_BLOCK):
        r0 = blk * GLA_BLOCK
        r1 = r0 + GLA_BLOCK
        nn = -(-r1 // LANES) * LANES
        if blk == 0:
            bq = bt[0:GLA_BLOCK]
            ek = jnp.exp(-bt[0:nn])
        else:
            ref = bt[r0 - 1:r0, :]
            bq = bt[r0:r1] - ref
            ek = jnp.exp(ref - bt[0:nn])
        qb = q[r0:r1] * jnp.exp(bq)
        qs = jnp.where(headm, jnp.concatenate([qb] * GLA_HEADS, axis=0), 0.0)
        kb = k[0:nn] * ek
        sc = _dot_nt(qs, kb)
        cm = _iota((n, nn), 1) <= (_iota((n, nn), 0) & (GLA_BLOCK - 1)) + r0
        sc = jnp.where(cm, sc, 0.0)
        ov = _dot(sc, v[0:nn])
        outs.append(jnp.concatenate(
            [ov[h * GLA_BLOCK:(h + 1) * GLA_BLOCK, h * GLA_DV:(h + 1) * GLA_DV] for h in range(GLA_HEADS)],
            axis=1))
    o_intra = jnp.concatenate(outs, axis=0)
    st = st_ref[...]
    o_state = _dot_nt(q * jnp.exp(bt), st)
    bl = bt[n - 1:n, :]
    kh = k * jnp.exp(bl - bt)
    upd = _dot_tn(v, kh)
    bdm = (_iota((GLA_WIDTH, QK_COLS), 0) >> 7) == (_iota((GLA_WIDTH, QK_COLS), 1) >> 6)
    st_ref[...] = jnp.exp(bl) * st + jnp.where(bdm, upd, 0.0)
    return o_intra + o_state


def _p_mixer_kernel(x_ref, gmix_ref, win_ref, walpha_ref, balpha_ref, ggla_ref, gsgu_ref,
                    ws_ref, bs_ref, wout_ref, h_ref, s_ref, st_ref):
    t = pl.program_id(1)

    @pl.when(t == 0)
    def _():
        st_ref[...] = jnp.zeros_like(st_ref)

    x = x_ref[...]
    q, k, v, r, u, sv, logg = _in_proj(x, gmix_ref[...], win_ref[...], walpha_ref[...], balpha_ref[...])
    o = _gla_prompt_tile(q, k, v, logg, st_ref)
    u = _gelu(u)
    svn = _rms_heads(_gelu(sv), gsgu_ref[...], SGU_HEADS, SGU_DH)
    ri = _iota((TT, TT), 0)
    ci = _iota((TT, TT), 1)
    wm = ((ri >> 7) == (ci >> 7)) & (ci <= ri)
    zs = []
    for h in range(SGU_HEADS):
        w = jnp.where(wm, ws_ref[h].astype(F32), 0.0)
        zs.append(_dot(w, svn[:, h * SGU_DH:(h + 1) * SGU_DH]))
    z = jnp.concatenate(zs, axis=1) + bs_ref[...]
    h_ref[...] = _mix_out(x, o, r, u, z, ggla_ref[...], wout_ref)

    @pl.when(t == pl.num_programs(1) - 1)
    def _():
        s_full = st_ref[...].T
        for h in range(GLA_HEADS):
            s_ref[h] = s_full[h * GLA_DK:(h + 1) * GLA_DK, h * GLA_DV:(h + 1) * GLA_DV]


def _gla_sample_block(q, k, v, logg, sin_ref, sout_ref, row0):
    n = STOK
    ri = _iota((n, n), 0)
    ci = _iota((n, n), 1)
    same = (ri >> 2) == (ci >> 2)
    causal = same & (ci <= ri)
    parts = _split3(logg)
    b = _dot_exact_lhs(causal.astype(BF16), parts)
    bl = _dot_exact_lhs(same.astype(BF16), parts)
    qt = q * jnp.exp(b)
    kt = k * jnp.exp(-b)
    kh = k * jnp.exp(bl - b)
    headm = (_iota((GLA_HEADS * n, QK_COLS), 0) >> 7) == (_iota((GLA_HEADS * n, QK_COLS), 1) >> 6)
    qs = jnp.where(headm, jnp.concatenate([qt] * GLA_HEADS, axis=0), 0.0)
    sc = _dot_nt(qs, kt)
    tok = _iota((GLA_HEADS * n, n), 0) & (n - 1)
    col = _iota((GLA_HEADS * n, n), 1)
    sc = jnp.where(((tok >> 2) == (col >> 2)) & (col <= tok), sc, 0.0)
    ov = _dot(sc, v)
    o_intra = jnp.concatenate(
        [ov[h * n:(h + 1) * n, h * GLA_DV:(h + 1) * GLA_DV] for h in range(GLA_HEADS)], axis=1)

    nexp = SROWS * GLA_DK
    expand = ((_iota((GLA_DK, nexp), 1) & (GLA_DK - 1)) == _iota((GLA_DK, nexp), 0)).astype(BF16)
    expand_t = ((_iota((nexp, GLA_DK), 0) & (GLA_DK - 1)) == _iota((nexp, GLA_DK), 1)).astype(BF16)
    bd = (_iota((n, nexp), 0) >> 2) == (_iota((n, nexp), 1) >> 6)
    bd_t = (_iota((nexp, n), 0) >> 6) == (_iota((nexp, n), 1) >> 2)
    last_tok = (_iota((n, QK_COLS), 0) & 3) == 3
    dec_t = jnp.where(last_tok, jnp.exp(bl), 0.0).T
    kh_t = kh.T
    ones = jnp.ones((n, GLA_DV), BF16)
    outs = []
    for h in range(GLA_HEADS):
        dsl = slice(h * GLA_DK, (h + 1) * GLA_DK)
        s_in = sin_ref[row0:row0 + SROWS, h].reshape(nexp, GLA_DV)
        qe = jnp.where(bd, _dot(qt[:, dsl], expand), 0.0)
        outs.append(_dot(qe, s_in))
        ke_t = jnp.where(bd_t, _dot(expand_t, kh_t[dsl, :]), 0.0)
        upd = _dot(ke_t, v[:, h * GLA_DV:(h + 1) * GLA_DV])
        dcol = 0.0
        for piece in _split3(dec_t[dsl, :]):
            de_t = jnp.where(bd_t, jnp.dot(expand_t, piece, preferred_element_type=F32), 0.0)
            dcol = dcol + jnp.dot(de_t.astype(BF16), ones, preferred_element_type=F32)
        s_new = dcol * s_in + upd
        sout_ref[row0:row0 + SROWS, h] = s_new.reshape(SROWS, GLA_DK, GLA_DV)
    return o_intra + jnp.concatenate(outs, axis=1)


def _s_mixer_kernel(x_ref, sin_ref, gmix_ref, win_ref, walpha_ref, balpha_ref, ggla_ref, gsgu_ref,
                    ws_ref, bs_ref, wout_ref, gx_ref, wq_ref,
                    h_ref, q_ref, sv_ref, sout_ref):
    x = x_ref[...]
    q, k, v, r, u, sv, logg = _in_proj(x, gmix_ref[...], win_ref[...], walpha_ref[...], balpha_ref[...])
    u = _gelu(u)
    svn = _rms_heads(_gelu(sv), gsgu_ref[...], SGU_HEADS, SGU_DH)
    sv_ref[...] = svn
    ri = _iota((STOK, STOK), 0)
    ci = _iota((STOK, STOK), 1)
    wm = ((ri >> 2) == (ci >> 2)) & (ci <= ri)
    os_, zs = [], []
    for sb in range(TT // STOK):
        ts = slice(sb * STOK, (sb + 1) * STOK)
        os_.append(_gla_sample_block(q[ts], k[ts], v[ts], logg[ts], sin_ref, sout_ref, sb * SROWS))
        zh = []
        for h in range(SGU_HEADS):
            w = jnp.where(wm, ws_ref[h].astype(F32), 0.0)
            zh.append(_dot(w, svn[ts, h * SGU_DH:(h + 1) * SGU_DH]))
        zs.append(jnp.concatenate(zh, axis=1) + bs_ref[...])
    o = jnp.concatenate(os_, axis=0)
    z = jnp.concatenate(zs, axis=0)
    h1 = _mix_out(x, o, r, u, z, ggla_ref[...], wout_ref)
    h_ref[...] = h1
    q_ref[...] = _dot(_rms(h1, gx_ref[...]), wq_ref[...])


def _softmax_rows(s):
    m = jnp.max(s, axis=-1, keepdims=True)
    e = jnp.exp(s - m)
    return e / jnp.sum(e, axis=-1, keepdims=True)


def _memkv_kernel(mem_ref, gmem_ref, wk_ref, wv_ref, mk_ref, mv_ref):
    mn = _rms(mem_ref[...], gmem_ref[...])
    mk_ref[...] = _dot(mn, wk_ref[...])
    mv_ref[...] = _dot(mn, wv_ref[...])


def _p_attn_kernel(h_ref, gx_ref, wq_ref, wo_ref, mk_ref, mv_ref, o_ref):
    h1 = h_ref[...]
    q = _dot(_rms(h1, gx_ref[...]), wq_ref[...])
    outs = []
    for h in range(MEM_HEADS):
        sl = slice(h * MEM_DH, (h + 1) * MEM_DH)
        s = _dot_nt(q[:, sl], mk_ref[:, sl]) * (MEM_DH ** -0.5)
        outs.append(_dot(_softmax_rows(s), mv_ref[:, sl]))
    o_ref[...] = h1 + _dot(jnp.concatenate(outs, axis=1), wo_ref[...])


def _s_attn_kernel(q_ref, k_ref, v_ref, o_ref):
    for r in range(ATT_ROWS):
        q = q_ref[r]
        for h in range(MEM_HEADS):
            sl = slice(h * MEM_DH, (h + 1) * MEM_DH)
            s = _dot_nt(q[:, sl], k_ref[r, :, sl]) * (MEM_DH ** -0.5)
            o_ref[r, :, sl] = _dot(_softmax_rows(s), v_ref[r, :, sl])


def _ffn_chunks(hn, wg_ref, wu_ref, cw_ref, cb_ref, wd_ref, shifts, sink):
    hb = hn.astype(BF16)
    acc = jnp.zeros((hn.shape[0], D_MODEL), F32)
    for c in range(D_FF // FF_CHUNK):
        cs = slice(c * FF_CHUNK, (c + 1) * FF_CHUNK)
        g = jnp.dot(hb, wg_ref[:, cs], preferred_element_type=F32)
        s1, s2 = shifts(g, cs)
        conv = cb_ref[:, cs] + ((cw_ref[0:1, cs] * s2 + cw_ref[1:2, cs] * s1) + cw_ref[2:3, cs] * g)
        up = jnp.dot(hb, wu_ref[:, cs], preferred_element_type=F32)
        acc = acc + _dot(_gelu(conv) * up, wd_ref[cs, :])
        sink(g, cs)
    return acc


def _p_ffn_kernel(h_ref, gffn_ref, wg_ref, wu_ref, cw_ref, cb_ref, wd_ref, gfin_ref,
                  y_ref, tail_ref, carry_ref):
    t = pl.program_id(1)

    @pl.when(t == 0)
    def _():
        carry_ref[...] = jnp.zeros_like(carry_ref)

    h2 = h_ref[...]
    row = _iota((TT, FF_CHUNK), 0)

    def shifts(g, cs):
        prev = carry_ref[:, cs]
        p1 = prev[7:8, :]
        p2 = prev[6:7, :]
        s1 = jnp.where(row == 0, p1, pltpu.roll(g, 1, 0))
        s2 = jnp.where(row == 0, p2, jnp.where(row == 1, p1, pltpu.roll(g, 2, 0)))
        return s1, s2

    def sink(g, cs):
        carry_ref[:, cs] = g[TT - 8:TT, :]

    f = _ffn_chunks(_rms(h2, gffn_ref[...]), wg_ref, wu_ref, cw_ref, cb_ref, wd_ref, shifts, sink)
    y_ref[...] = _rms(h2 + f, gfin_ref[...])

    @pl.when(t == pl.num_programs(1) - 1)
    def _():
        tail_ref[...] = carry_ref[...]


def _s_ffn_kernel(h_ref, ao_ref, wo_ref, bufa_ref, bufb_ref, gffn_ref, wg_ref, wu_ref, cw_ref, cb_ref,
                  wd_ref, gfin_ref, y_ref, ta_ref, tb_ref):
    h2 = h_ref[...] + _dot(ao_ref[...], wo_ref[...])
    tpos = _iota((TT, FF_CHUNK), 0) & 3

    def shifts(g, cs):
        s1 = jnp.where(tpos == 0, bufa_ref[:, cs], pltpu.roll(g, 1, 0))
        s2 = jnp.where(tpos < 2, bufb_ref[:, cs], pltpu.roll(g, 2, 0))
        return s1, s2

    srow = _iota((TT // 4, TT), 0) * 4
    scol = _iota((TT // 4, TT), 1)
    sel_a = (scol == srow + 2).astype(BF16)
    sel_b = (scol == srow + 3).astype(BF16)

    def sink(g, cs):
        parts = _split3(g)
        ta_ref[:, cs] = _dot_exact_lhs(sel_a, parts)
        tb_ref[:, cs] = _dot_exact_lhs(sel_b, parts)

    f = _ffn_chunks(_rms(h2, gffn_ref[...]), wg_ref, wu_ref, cw_ref, cb_ref, wd_ref, shifts, sink)
    y_ref[...] = _rms(h2 + f, gfin_ref[...])


def _const_spec(shape, single=True):
    nd = len(shape)
    kw = {"pipeline_mode": pl.Buffered(1)} if single else {}
    return pl.BlockSpec(shape, lambda *_: (0,) * nd, **kw)


def _params(sem):
    return pltpu.CompilerParams(dimension_semantics=sem, vmem_limit_bytes=VMEM_LIMIT)


def kernel(x_prompt, x_sample, mem_prompt, state_gla, state_conv, cache_mem_k, cache_mem_v, g_mix, w_in, w_alpha, b_alpha, g_gla_out, g_sgu, w_s, b_s, w_out, g_x, g_mem, wq_x, wk_x, wv_x, wo_x, g_ffn, w_gate, w_up, conv_w, conv_b, w_down, g_final):
    B, T, _ = x_prompt.shape
    RB, TS, _ = x_sample.shape
    assert T % TT == 0 and TS == 4 and (RB * TS) % TT == 0 and RB % ATT_ROWS == 0
    nt = T // TT
    ntok_s = RB * TS
    rows_per_step = TT // TS

    w_in0 = w_in[0]
    cut = 2 * QK_COLS + 2 * GLA_WIDTH
    w_in_r = jnp.concatenate(
        [w_in0[:, :cut], w_in0[:, cut + GLA_RANK:], w_in0[:, cut:cut + GLA_RANK],
         jnp.zeros((D_MODEL, RANK_PAD - GLA_RANK), F32)], axis=1).astype(BF16)
    w_alpha_p = jnp.concatenate([w_alpha[0], jnp.zeros((RANK_PAD - GLA_RANK, QK_COLS), F32)], axis=0).astype(BF16)
    b_alpha_r = b_alpha[0].reshape(1, QK_COLS)
    gmix = g_mix[0].reshape(1, D_MODEL)
    ggla = g_gla_out[0].reshape(1, GLA_WIDTH)
    gsgu = g_sgu[0].reshape(1, SGU_WIDTH)
    gx = g_x[0].reshape(1, D_MODEL)
    gmem = g_mem[0].reshape(1, D_MODEL)
    gffn = g_ffn[0].reshape(1, D_MODEL)
    gfin = g_final.reshape(1, D_MODEL)
    w_out_b = w_out[0].astype(BF16)
    wq_b, wk_b, wv_b, wo_b = (w[0].astype(BF16) for w in (wq_x, wk_x, wv_x, wo_x))
    wg_b, wu_b, wd_b = (w[0].astype(BF16) for w in (w_gate, w_up, w_down))
    cw = conv_w[0]
    cb = conv_b[0].reshape(1, D_FF)
    ws_p = jnp.tile(w_s[0], (1, TT // SGU_CHUNK, TT // SGU_CHUNK)).astype(BF16)
    bs_p = jnp.tile(jnp.repeat(b_s[0].T, SGU_DH, axis=1), (TT // SGU_CHUNK, 1))
    ws_s = jnp.tile(w_s[0][:, :TS, :TS], (1, SROWS, SROWS)).astype(BF16)
    bs_s = jnp.tile(jnp.repeat(b_s[0][:, :TS].T, SGU_DH, axis=1), (SROWS, 1))

    mixer_w_specs = [
        _const_spec((1, D_MODEL)), _const_spec((D_MODEL, D_IN_PAD)), _const_spec((RANK_PAD, QK_COLS)),
        _const_spec((1, QK_COLS)), _const_spec((1, GLA_WIDTH)), _const_spec((1, SGU_WIDTH)),
    ]

    mk_p, mv_p = pl.pallas_call(
        _memkv_kernel,
        grid=(B,),
        in_specs=[pl.BlockSpec((None, N_MEM, D_MODEL), lambda b: (b, 0, 0)),
                  _const_spec((1, D_MODEL)), _const_spec((D_MODEL, D_MODEL)), _const_spec((D_MODEL, D_MODEL))],
        out_specs=[pl.BlockSpec((None, N_MEM, D_MODEL), lambda b: (b, 0, 0))] * 2,
        out_shape=[jax.ShapeDtypeStruct((B, N_MEM, D_MODEL), F32)] * 2,
        compiler_params=_params(("arbitrary",)),
        name="memkv",
    )(mem_prompt, gmem, wk_b, wv_b)

    tile_spec = pl.BlockSpec((None, TT, D_MODEL), lambda b, t: (b, t, 0))
    h1_p, s_p = pl.pallas_call(
        _p_mixer_kernel,
        grid=(B, nt),
        in_specs=[tile_spec] + mixer_w_specs + [
            _const_spec((SGU_HEADS, TT, TT)), _const_spec((TT, SGU_WIDTH)), _const_spec((D_MODEL, D_MODEL))],
        out_specs=[tile_spec,
                   pl.BlockSpec((None, GLA_HEADS, GLA_DK, GLA_DV), lambda b, t: (b, 0, 0, 0))],
        out_shape=[jax.ShapeDtypeStruct((B, T, D_MODEL), F32),
                   jax.ShapeDtypeStruct((B, GLA_HEADS, GLA_DK, GLA_DV), F32)],
        scratch_shapes=[pltpu.VMEM((GLA_WIDTH, QK_COLS), F32)],
        compiler_params=_params(("arbitrary", "arbitrary")),
        name="p_mixer",
    )(x_prompt, gmix, w_in_r, w_alpha_p, b_alpha_r, ggla, gsgu, ws_p, bs_p, w_out_b)

    kv_spec = pl.BlockSpec((None, N_MEM, D_MODEL), lambda b, t: (b, 0, 0))
    h2_p = pl.pallas_call(
        _p_attn_kernel,
        grid=(B, nt),
        in_specs=[tile_spec, _const_spec((1, D_MODEL)), _const_spec((D_MODEL, D_MODEL)),
                  _const_spec((D_MODEL, D_MODEL)), kv_spec, kv_spec],
        out_specs=tile_spec,
        out_shape=jax.ShapeDtypeStruct((B, T, D_MODEL), F32),
        compiler_params=_params(("arbitrary", "arbitrary")),
        name="p_attn",
    )(h1_p, gx, wq_b, wo_b, mk_p, mv_p)

    ffn_w_specs = [_const_spec((1, D_MODEL)), _const_spec((D_MODEL, D_FF)), _const_spec((D_MODEL, D_FF)),
                   _const_spec((CONV_W, D_FF)), _const_spec((1, D_FF)), _const_spec((D_FF, D_MODEL)),
                   _const_spec((1, D_MODEL))]
    y_p, tail_p = pl.pallas_call(
        _p_ffn_kernel,
        grid=(B, nt),
        in_specs=[tile_spec] + ffn_w_specs,
        out_specs=[tile_spec, pl.BlockSpec((None, 8, D_FF), lambda b, t: (b, 0, 0))],
        out_shape=[jax.ShapeDtypeStruct((B, T, D_MODEL), F32), jax.ShapeDtypeStruct((B, 8, D_FF), F32)],
        scratch_shapes=[pltpu.VMEM((8, D_FF), F32)],
        compiler_params=_params(("arbitrary", "arbitrary")),
        name="p_ffn",
    )(h2_p, gffn, wg_b, wu_b, cw, cb, wd_b, gfin)

    xs = x_sample.reshape(ntok_s, D_MODEL)
    s_in = state_gla[0]
    tok_spec = pl.BlockSpec((TT, D_MODEL), lambda i: (i, 0))
    st_spec = pl.BlockSpec((rows_per_step, GLA_HEADS, GLA_DK, GLA_DV), lambda i: (i, 0, 0, 0))
    h1_s, q_s, sv_s, s_s = pl.pallas_call(
        _s_mixer_kernel,
        grid=(ntok_s // TT,),
        in_specs=[tok_spec, st_spec] + mixer_w_specs + [
            _const_spec((SGU_HEADS, STOK, STOK)), _const_spec((STOK, SGU_WIDTH)), _const_spec((D_MODEL, D_MODEL)),
            _const_spec((1, D_MODEL)), _const_spec((D_MODEL, D_MODEL))],
        out_specs=[tok_spec, tok_spec, pl.BlockSpec((TT, SGU_WIDTH), lambda i: (i, 0)), st_spec],
        out_shape=[jax.ShapeDtypeStruct((ntok_s, D_MODEL), F32), jax.ShapeDtypeStruct((ntok_s, D_MODEL), F32),
                   jax.ShapeDtypeStruct((ntok_s, SGU_WIDTH), F32),
                   jax.ShapeDtypeStruct((RB, GLA_HEADS, GLA_DK, GLA_DV), F32)],
        compiler_params=_params(("arbitrary",)),
        name="s_mixer",
    )(xs, s_in, gmix, w_in_r, w_alpha_p, b_alpha_r, ggla, gsgu, ws_s, bs_s, w_out_b, gx, wq_b)

    ck = cache_mem_k[0].reshape(RB, N_MEM, D_MODEL)
    cv = cache_mem_v[0].reshape(RB, N_MEM, D_MODEL)
    q3_spec = pl.BlockSpec((ATT_ROWS, TS, D_MODEL), lambda i: (i, 0, 0))
    ckv_spec = pl.BlockSpec((ATT_ROWS, N_MEM, D_MODEL), lambda i: (i, 0, 0))
    ao_s = pl.pallas_call(
        _s_attn_kernel,
        grid=(RB // ATT_ROWS,),
        in_specs=[q3_spec, ckv_spec, ckv_spec],
        out_specs=q3_spec,
        out_shape=jax.ShapeDtypeStruct((RB, TS, D_MODEL), F32),
        compiler_params=_params(("arbitrary",)),
        name="s_attn",
    )(q_s.reshape(RB, TS, D_MODEL), ck, cv)

    buf = state_conv[0]
    buf_a = jnp.pad(buf[:, 1:2], ((0, 0), (0, TS - 1), (0, 0))).reshape(ntok_s, D_FF)
    buf_b = jnp.pad(buf, ((0, 0), (0, TS - 2), (0, 0))).reshape(ntok_s, D_FF)
    ff_tok = pl.BlockSpec((TT, D_FF), lambda i: (i, 0))
    tail_spec = pl.BlockSpec((rows_per_step, D_FF), lambda i: (i, 0))
    y_s, ta_s, tb_s = pl.pallas_call(
        _s_ffn_kernel,
        grid=(ntok_s // TT,),
        in_specs=[tok_spec, tok_spec, _const_spec((D_MODEL, D_MODEL)), ff_tok, ff_tok] + ffn_w_specs,
        out_specs=[tok_spec, tail_spec, tail_spec],
        out_shape=[jax.ShapeDtypeStruct((ntok_s, D_MODEL), F32),
                   jax.ShapeDtypeStruct((RB, D_FF), F32), jax.ShapeDtypeStruct((RB, D_FF), F32)],
        compiler_params=_params(("arbitrary",)),
        name="s_ffn",
    )(h1_s, ao_s.reshape(ntok_s, D_MODEL), wo_b, buf_a, buf_b, gffn, wg_b, wu_b, cw, cb, wd_b, gfin)

    return (y_p,
            y_s.reshape(RB, TS, D_MODEL),
            s_p[None],
            tail_p[:, 6:8][None],
            mk_p.reshape(1, B, N_MEM, MEM_HEADS, MEM_DH),
            mv_p.reshape(1, B, N_MEM, MEM_HEADS, MEM_DH),
            s_s[None],
            jnp.stack([ta_s, tb_s], axis=1)[None],
            sv_s.reshape(1, RB, TS, SGU_HEADS, SGU_DH))
```

```python
import functools

import jax
import jax.numpy as jnp
from jax import lax
from jax.experimental import pallas as pl
from jax.experimental.pallas import tpu as pltpu

F32 = jnp.float32
BF16 = jnp.bfloat16

D_MODEL = 1024
GLA_HEADS = 4
GLA_DK = 64
GLA_DV = 128
QK_COLS = GLA_HEADS * GLA_DK
GLA_WIDTH = GLA_HEADS * GLA_DV
GLA_RANK = 16
GLA_TAU = 16.0
SGU_HEADS = 4
SGU_DH = 128
SGU_WIDTH = SGU_HEADS * SGU_DH
SGU_CHUNK = 128
N_MEM = 256
MEM_HEADS = 4
MEM_DH = 256
D_FF = 2816
CONV_W = 3
EPS = 1e-6

LANES = 128
RANK_PAD = LANES
D_IN_PAD = 2 * QK_COLS + 2 * GLA_WIDTH + 2 * SGU_WIDTH + RANK_PAD
TT = 256
GLA_BLOCK = 64
SROWS = 32
STOK = SROWS * 4
FF_CHUNK = 256
ATT_ROWS = 8
VMEM_LIMIT = 56 * 1024 * 1024


def _dot(a, b):
    return jnp.dot(a.astype(BF16), b.astype(BF16), preferred_element_type=F32)


def _dot_nt(a, b):
    return lax.dot_general(a.astype(BF16), b.astype(BF16), (((1,), (1,)), ((), ())),
                           preferred_element_type=F32)


def _dot_tn(a, b):
    return lax.dot_general(a.astype(BF16), b.astype(BF16), (((0,), (0,)), ((), ())),
                           preferred_element_type=F32)


def _split3(x):
    hi = x.astype(BF16)
    r1 = x - hi.astype(F32)
    mid = r1.astype(BF16)
    lo = (r1 - mid.astype(F32)).astype(BF16)
    return hi, mid, lo


def _dot_exact_lhs(sel, parts):
    hi, mid, lo = parts
    return (jnp.dot(sel, hi, preferred_element_type=F32)
            + jnp.dot(sel, mid, preferred_element_type=F32)
            + jnp.dot(sel, lo, preferred_element_type=F32))


def _rms(x, g):
    ms = jnp.mean(x * x, axis=-1, keepdims=True)
    return x * lax.rsqrt(ms + EPS) * g


def _rms_heads(x, g, nh, dh):
    outs = []
    for h in range(nh):
        xh = x[:, h * dh:(h + 1) * dh]
        ms = jnp.mean(xh * xh, axis=-1, keepdims=True)
        outs.append(xh * lax.rsqrt(ms + EPS))
    return jnp.concatenate(outs, axis=-1) * g


def _gelu(x):
    c = 0.7978845608028654
    return x * (0.5 * (1.0 + jnp.tanh(c * (x + 0.044715 * (x * x * x)))))


def _silu(x):
    return x * (1.0 / (1.0 + jnp.exp(-x)))


def _log_sigmoid(x):
    return jnp.minimum(x, 0.0) - jnp.log1p(jnp.exp(-jnp.abs(x)))


def _iota(shape, dim):
    return lax.broadcasted_iota(jnp.int32, shape, dim)


def _in_proj(x, gmix, w_in, w_alpha, b_alpha):
    xn = _rms(x, gmix)
    p = _dot(xn, w_in)
    q = p[:, 0:256] * (GLA_DK ** -0.5)
    k = p[:, 256:512]
    v = p[:, 512:1024]
    r = p[:, 1024:1536]
    u = p[:, 1536:2048]
    sv = p[:, 2048:2560]
    a = p[:, 2560:2688]
    xg = _dot(a, w_alpha) + b_alpha
    logg = _log_sigmoid(xg) * (1.0 / GLA_TAU)
    return q, k, v, r, u, sv, logg


def _mix_out(x, o, r, u, z, ggla, w_out):
    og = _rms_heads(o, ggla, GLA_HEADS, GLA_DV) * _silu(r)
    s_out = u * z
    y = _dot(og, w_out[0:GLA_WIDTH, :]) + _dot(s_out, w_out[GLA_WIDTH:, :])
    return x + y


def _gla_prompt_tile(q, k, v, logg, st_ref):
    n = TT
    ri = _iota((n, n), 0)
    ci = _iota((n, n), 1)
    low = (ci <= ri).astype(BF16)
    bt = _dot_exact_lhs(low, _split3(logg))
    headm = (ri >> 6) == (ci >> 6)
    outs = []
    for blk in range(n // GLA_BLOCK):
        r0 = blk * GLA_BLOCK
        r1 = r0 + GLA_BLOCK
        nn = -(-r1 // LANES) * LANES
        if blk == 0:
            bq = bt[0:GLA_BLOCK]
            ek = jnp.exp(-bt[0:nn])
        else:
            ref = bt[r0 - 1:r0, :]
            bq = bt[r0:r1] - ref
            ek = jnp.exp(ref - bt[0:nn])
        qb = q[r0:r1] * jnp.exp(bq)
        qs = jnp.where(headm, jnp.concatenate([qb] * GLA_HEADS, axis=0), 0.0)
        kb = k[0:nn] * ek
        sc = _dot_nt(qs, kb)
        cm = _iota((n, nn), 1) <= (_iota((n, nn), 0) & (GLA_BLOCK - 1)) + r0
        sc = jnp.where(cm, sc, 0.0)
        ov = _dot(sc, v[0:nn])
        outs.append(jnp.concatenate(
            [ov[h * GLA_BLOCK:(h + 1) * GLA_BLOCK, h * GLA_DV:(h + 1) * GLA_DV] for h in range(GLA_HEADS)],
            axis=1))
    o_intra = jnp.concatenate(outs, axis=0)
    st = st_ref[...]
    o_state = _dot_nt(q * jnp.exp(bt), st)
    bl = bt[n - 1:n, :]
    kh = k * jnp.exp(bl - bt)
    upd = _dot_tn(v, kh)
    bdm = (_iota((GLA_WIDTH, QK_COLS), 0) >> 7) == (_iota((GLA_WIDTH, QK_COLS), 1) >> 6)
    st_ref[...] = jnp.exp(bl) * st + jnp.where(bdm, upd, 0.0)
    return o_intra + o_state


def _p_mixer_kernel(x_ref, gmix_ref, win_ref, walpha_ref, balpha_ref, ggla_ref, gsgu_ref,
                    ws_ref, bs_ref, wout_ref, h_ref, s_ref, st_ref):
    t = pl.program_id(1)

    @pl.when(t == 0)
    def _():
        st_ref[...] = jnp.zeros_like(st_ref)

    x = x_ref[...]
    q, k, v, r, u, sv, logg = _in_proj(x, gmix_ref[...], win_ref[...], walpha_ref[...], balpha_ref[...])
    o = _gla_prompt_tile(q, k, v, logg, st_ref)
    u = _gelu(u)
    svn = _rms_heads(_gelu(sv), gsgu_ref[...], SGU_HEADS, SGU_DH)
    ri = _iota((TT, TT), 0)
    ci = _iota((TT, TT), 1)
    wm = ((ri >> 7) == (ci >> 7)) & (ci <= ri)
    zs = []
    for h in range(SGU_HEADS):
        w = jnp.where(wm, ws_ref[h].astype(F32), 0.0)
        zs.append(_dot(w, svn[:, h * SGU_DH:(h + 1) * SGU_DH]))
    z = jnp.concatenate(zs, axis=1) + bs_ref[...]
    h_ref[...] = _mix_out(x, o, r, u, z, ggla_ref[...], wout_ref)

    @pl.when(t == pl.num_programs(1) - 1)
    def _():
        s_full = st_ref[...].T
        for h in range(GLA_HEADS):
            s_ref[h] = s_full[h * GLA_DK:(h + 1) * GLA_DK, h * GLA_DV:(h + 1) * GLA_DV]


def _gla_sample_block(q, k, v, logg, sin_ref, sout_ref, row0):
    n = STOK
    ri = _iota((n, n), 0)
    ci = _iota((n, n), 1)
    same = (ri >> 2) == (ci >> 2)
    causal = same & (ci <= ri)
    parts = _split3(logg)
    b = _dot_exact_lhs(causal.astype(BF16), parts)
    bl = _dot_exact_lhs(same.astype(BF16), parts)
    qt = q * jnp.exp(b)
    kt = k * jnp.exp(-b)
    kh = k * jnp.exp(bl - b)
    headm = (_iota((GLA_HEADS * n, QK_COLS), 0) >> 7) == (_iota((GLA_HEADS * n, QK_COLS), 1) >> 6)
    qs = jnp.where(headm, jnp.concatenate([qt] * GLA_HEADS, axis=0), 0.0)
    sc = _dot_nt(qs, kt)
    tok = _iota((GLA_HEADS * n, n), 0) & (n - 1)
    col = _iota((GLA_HEADS * n, n), 1)
    sc = jnp.where(((tok >> 2) == (col >> 2)) & (col <= tok), sc, 0.0)
    ov = _dot(sc, v)
    o_intra = jnp.concatenate(
        [ov[h * n:(h + 1) * n, h * GLA_DV:(h + 1) * GLA_DV] for h in range(GLA_HEADS)], axis=1)

    nexp = SROWS * GLA_DK
    expand = ((_iota((GLA_DK, nexp), 1) & (GLA_DK - 1)) == _iota((GLA_DK, nexp), 0)).astype(BF16)
    expand_t = ((_iota((nexp, GLA_DK), 0) & (GLA_DK - 1)) == _iota((nexp, GLA_DK), 1)).astype(BF16)
    bd = (_iota((n, nexp), 0) >> 2) == (_iota((n, nexp), 1) >> 6)
    bd_t = (_iota((nexp, n), 0) >> 6) == (_iota((nexp, n), 1) >> 2)
    last_tok = (_iota((n, QK_COLS), 0) & 3) == 3
    dec_t = jnp.where(last_tok, jnp.exp(bl), 0.0).T
    kh_t = kh.T
    ones = jnp.ones((n, GLA_DV), BF16)
    outs = []
    for h in range(GLA_HEADS):
        dsl = slice(h * GLA_DK, (h + 1) * GLA_DK)
        s_in = sin_ref[row0:row0 + SROWS, h].reshape(nexp, GLA_DV)
        qe = jnp.where(bd, _dot(qt[:, dsl], expand), 0.0)
        outs.append(_dot(qe, s_in))
        ke_t = jnp.where(bd_t, _dot(expand_t, kh_t[dsl, :]), 0.0)
        upd = _dot(ke_t, v[:, h * GLA_DV:(h + 1) * GLA_DV])
        dcol = 0.0
        for piece in _split3(dec_t[dsl, :]):
            de_t = jnp.where(bd_t, jnp.dot(expand_t, piece, preferred_element_type=F32), 0.0)
            dcol = dcol + jnp.dot(de_t.astype(BF16), ones, preferred_element_type=F32)
        s_new = dcol * s_in + upd
        sout_ref[row0:row0 + SROWS, h] = s_new.reshape(SROWS, GLA_DK, GLA_DV)
    return o_intra + jnp.concatenate(outs, axis=1)


def _s_mixer_kernel(x_ref, sin_ref, gmix_ref, win_ref, walpha_ref, balpha_ref, ggla_ref, gsgu_ref,
                    ws_ref, bs_ref, wout_ref, gx_ref, wq_ref,
                    h_ref, q_ref, sv_ref, sout_ref):
    x = x_ref[...]
    q, k, v, r, u, sv, logg = _in_proj(x, gmix_ref[...], win_ref[...], walpha_ref[...], balpha_ref[...])
    u = _gelu(u)
    svn = _rms_heads(_gelu(sv), gsgu_ref[...], SGU_HEADS, SGU_DH)
    sv_ref[...] = svn
    ri = _iota((STOK, STOK), 0)
    ci = _iota((STOK, STOK), 1)
    wm = ((ri >> 2) == (ci >> 2)) & (ci <= ri)
    os_, zs = [], []
    for sb in range(TT // STOK):
        ts = slice(sb * STOK, (sb + 1) * STOK)
        os_.append(_gla_sample_block(q[ts], k[ts], v[ts], logg[ts], sin_ref, sout_ref, sb * SROWS))
        zh = []
        for h in range(SGU_HEADS):
            w = jnp.where(wm, ws_ref[h].astype(F32), 0.0)
            zh.append(_dot(w, svn[ts, h * SGU_DH:(h + 1) * SGU_DH]))
        zs.append(jnp.concatenate(zh, axis=1) + bs_ref[...])
    o = jnp.concatenate(os_, axis=0)
    z = jnp.concatenate(zs, axis=0)
    h1 = _mix_out(x, o, r, u, z, ggla_ref[...], wout_ref)
    h_ref[...] = h1
    q_ref[...] = _dot(_rms(h1, gx_ref[...]), wq_ref[...])


def _softmax_rows(s):
    m = jnp.max(s, axis=-1, keepdims=True)
    e = jnp.exp(s - m)
    return e / jnp.sum(e, axis=-1, keepdims=True)


def _memkv_kernel(mem_ref, gmem_ref, wk_ref, wv_ref, mk_ref, mv_ref):
    mn = _rms(mem_ref[...], gmem_ref[...])
    mk_ref[...] = _dot(mn, wk_ref[...])
    mv_ref[...] = _dot(mn, wv_ref[...])


def _p_attn_kernel(h_ref, gx_ref, wq_ref, wo_ref, mk_ref, mv_ref, o_ref):
    h1 = h_ref[...]
    q = _dot(_rms(h1, gx_ref[...]), wq_ref[...])
    outs = []
    for h in range(MEM_HEADS):
        sl = slice(h * MEM_DH, (h + 1) * MEM_DH)
        s = _dot_nt(q[:, sl], mk_ref[:, sl]) * (MEM_DH ** -0.5)
        outs.append(_dot(_softmax_rows(s), mv_ref[:, sl]))
    o_ref[...] = h1 + _dot(jnp.concatenate(outs, axis=1), wo_ref[...])


def _s_attn_kernel(q_ref, k_ref, v_ref, o_ref):
    nq = MEM_HEADS * 4
    nk = N_MEM * MEM_HEADS
    own_head = (_iota((nq, nk), 1) & (MEM_HEADS - 1)) == (_iota((nq, nk), 0) >> 2)
    for r in range(ATT_ROWS):
        q = q_ref[r]
        qh = jnp.concatenate([q[:, h * MEM_DH:(h + 1) * MEM_DH] for h in range(MEM_HEADS)], axis=0)
        kk = k_ref[r].reshape(nk, MEM_DH)
        vv = v_ref[r].reshape(nk, MEM_DH)
        s = jnp.where(own_head, _dot_nt(qh, kk) * (MEM_DH ** -0.5), -jnp.inf)
        o = _dot(_softmax_rows(s), vv)
        for h in range(MEM_HEADS):
            o_ref[r, :, h * MEM_DH:(h + 1) * MEM_DH] = o[4 * h:4 * h + 4]


def _ffn_chunks(hn, wg_ref, wu_ref, cw_ref, cb_ref, wd_ref, shifts, sink):
    hb = hn.astype(BF16)
    acc = jnp.zeros((hn.shape[0], D_MODEL), F32)
    for c in range(D_FF // FF_CHUNK):
        cs = slice(c * FF_CHUNK, (c + 1) * FF_CHUNK)
        g = jnp.dot(hb, wg_ref[:, cs], preferred_element_type=F32)
        s1, s2 = shifts(g, cs)
        conv = cb_ref[:, cs] + ((cw_ref[0:1, cs] * s2 + cw_ref[1:2, cs] * s1) + cw_ref[2:3, cs] * g)
        up = jnp.dot(hb, wu_ref[:, cs], preferred_element_type=F32)
        acc = acc + _dot(_gelu(conv) * up, wd_ref[cs, :])
        sink(g, cs)
    return acc


def _p_ffn_kernel(h_ref, gffn_ref, wg_ref, wu_ref, cw_ref, cb_ref, wd_ref, gfin_ref,
                  y_ref, tail_ref, carry_ref):
    t = pl.program_id(1)

    @pl.when(t == 0)
    def _():
        carry_ref[...] = jnp.zeros_like(carry_ref)

    h2 = h_ref[...]
    row = _iota((TT, FF_CHUNK), 0)

    def shifts(g, cs):
        prev = carry_ref[:, cs]
        p1 = prev[7:8, :]
        p2 = prev[6:7, :]
        s1 = jnp.where(row == 0, p1, pltpu.roll(g, 1, 0))
        s2 = jnp.where(row == 0, p2, jnp.where(row == 1, p1, pltpu.roll(g, 2, 0)))
        return s1, s2

    def sink(g, cs):
        carry_ref[:, cs] = g[TT - 8:TT, :]

    f = _ffn_chunks(_rms(h2, gffn_ref[...]), wg_ref, wu_ref, cw_ref, cb_ref, wd_ref, shifts, sink)
    y_ref[...] = _rms(h2 + f, gfin_ref[...])

    @pl.when(t == pl.num_programs(1) - 1)
    def _():
        tail_ref[...] = carry_ref[...]


def _s_ffn_kernel(h_ref, ao_ref, wo_ref, bufa_ref, bufb_ref, gffn_ref, wg_ref, wu_ref, cw_ref, cb_ref,
                  wd_ref, gfin_ref, y_ref, ta_ref, tb_ref):
    h2 = h_ref[...] + _dot(ao_ref[...], wo_ref[...])
    tpos = _iota((TT, FF_CHUNK), 0) & 3

    def shifts(g, cs):
        s1 = jnp.where(tpos == 0, bufa_ref[:, cs], pltpu.roll(g, 1, 0))
        s2 = jnp.where(tpos < 2, bufb_ref[:, cs], pltpu.roll(g, 2, 0))
        return s1, s2

    srow = _iota((TT // 4, TT), 0) * 4
    scol = _iota((TT // 4, TT), 1)
    sel_a = (scol == srow + 2).astype(BF16)
    sel_b = (scol == srow + 3).astype(BF16)

    def sink(g, cs):
        parts = _split3(g)
        ta_ref[:, cs] = _dot_exact_lhs(sel_a, parts)
        tb_ref[:, cs] = _dot_exact_lhs(sel_b, parts)

    f = _ffn_chunks(_rms(h2, gffn_ref[...]), wg_ref, wu_ref, cw_ref, cb_ref, wd_ref, shifts, sink)
    y_ref[...] = _rms(h2 + f, gfin_ref[...])


def _const_spec(shape, single=True):
    nd = len(shape)
    kw = {"pipeline_mode": pl.Buffered(1)} if single else {}
    return pl.BlockSpec(shape, lambda *_: (0,) * nd, **kw)


def _params(sem):
    return pltpu.CompilerParams(dimension_semantics=sem, vmem_limit_bytes=VMEM_LIMIT)


def kernel(x_prompt, x_sample, mem_prompt, state_gla, state_conv, cache_mem_k, cache_mem_v, g_mix, w_in, w_alpha, b_alpha, g_gla_out, g_sgu, w_s, b_s, w_out, g_x, g_mem, wq_x, wk_x, wv_x, wo_x, g_ffn, w_gate, w_up, conv_w, conv_b, w_down, g_final):
    B, T, _ = x_prompt.shape
    RB, TS, _ = x_sample.shape
    assert T % TT == 0 and TS == 4 and (RB * TS) % TT == 0 and RB % ATT_ROWS == 0
    nt = T // TT
    ntok_s = RB * TS
    rows_per_step = TT // TS

    w_in0 = w_in[0]
    cut = 2 * QK_COLS + 2 * GLA_WIDTH
    w_in_r = jnp.concatenate(
        [w_in0[:, :cut], w_in0[:, cut + GLA_RANK:], w_in0[:, cut:cut + GLA_RANK],
         jnp.zeros((D_MODEL, RANK_PAD - GLA_RANK), F32)], axis=1).astype(BF16)
    w_alpha_p = jnp.concatenate([w_alpha[0], jnp.zeros((RANK_PAD - GLA_RANK, QK_COLS), F32)], axis=0).astype(BF16)
    b_alpha_r = b_alpha[0].reshape(1, QK_COLS)
    gmix = g_mix[0].reshape(1, D_MODEL)
    ggla = g_gla_out[0].reshape(1, GLA_WIDTH)
    gsgu = g_sgu[0].reshape(1, SGU_WIDTH)
    gx = g_x[0].reshape(1, D_MODEL)
    gmem = g_mem[0].reshape(1, D_MODEL)
    gffn = g_ffn[0].reshape(1, D_MODEL)
    gfin = g_final.reshape(1, D_MODEL)
    w_out_b = w_out[0].astype(BF16)
    wq_b, wk_b, wv_b, wo_b = (w[0].astype(BF16) for w in (wq_x, wk_x, wv_x, wo_x))
    wg_b, wu_b, wd_b = (w[0].astype(BF16) for w in (w_gate, w_up, w_down))
    cw = conv_w[0]
    cb = conv_b[0].reshape(1, D_FF)
    ws_p = jnp.tile(w_s[0], (1, TT // SGU_CHUNK, TT // SGU_CHUNK)).astype(BF16)
    bs_p = jnp.tile(jnp.repeat(b_s[0].T, SGU_DH, axis=1), (TT // SGU_CHUNK, 1))
    ws_s = jnp.tile(w_s[0][:, :TS, :TS], (1, SROWS, SROWS)).astype(BF16)
    bs_s = jnp.tile(jnp.repeat(b_s[0][:, :TS].T, SGU_DH, axis=1), (SROWS, 1))

    mixer_w_specs = [
        _const_spec((1, D_MODEL)), _const_spec((D_MODEL, D_IN_PAD)), _const_spec((RANK_PAD, QK_COLS)),
        _const_spec((1, QK_COLS)), _const_spec((1, GLA_WIDTH)), _const_spec((1, SGU_WIDTH)),
    ]

    mk_p, mv_p = pl.pallas_call(
        _memkv_kernel,
        grid=(B,),
        in_specs=[pl.BlockSpec((None, N_MEM, D_MODEL), lambda b: (b, 0, 0)),
                  _const_spec((1, D_MODEL)), _const_spec((D_MODEL, D_MODEL)), _const_spec((D_MODEL, D_MODEL))],
        out_specs=[pl.BlockSpec((None, N_MEM, D_MODEL), lambda b: (b, 0, 0))] * 2,
        out_shape=[jax.ShapeDtypeStruct((B, N_MEM, D_MODEL), F32)] * 2,
        compiler_params=_params(("arbitrary",)),
        name="memkv",
    )(mem_prompt, gmem, wk_b, wv_b)

    tile_spec = pl.BlockSpec((None, TT, D_MODEL), lambda b, t: (b, t, 0))
    h1_p, s_p = pl.pallas_call(
        _p_mixer_kernel,
        grid=(B, nt),
        in_specs=[tile_spec] + mixer_w_specs + [
            _const_spec((SGU_HEADS, TT, TT)), _const_spec((TT, SGU_WIDTH)), _const_spec((D_MODEL, D_MODEL))],
        out_specs=[tile_spec,
                   pl.BlockSpec((None, GLA_HEADS, GLA_DK, GLA_DV), lambda b, t: (b, 0, 0, 0))],
        out_shape=[jax.ShapeDtypeStruct((B, T, D_MODEL), F32),
                   jax.ShapeDtypeStruct((B, GLA_HEADS, GLA_DK, GLA_DV), F32)],
        scratch_shapes=[pltpu.VMEM((GLA_WIDTH, QK_COLS), F32)],
        compiler_params=_params(("arbitrary", "arbitrary")),
        name="p_mixer",
    )(x_prompt, gmix, w_in_r, w_alpha_p, b_alpha_r, ggla, gsgu, ws_p, bs_p, w_out_b)

    kv_spec = pl.BlockSpec((None, N_MEM, D_MODEL), lambda b, t: (b, 0, 0))
    h2_p = pl.pallas_call(
        _p_attn_kernel,
        grid=(B, nt),
        in_specs=[tile_spec, _const_spec((1, D_MODEL)), _const_spec((D_MODEL, D_MODEL)),
                  _const_spec((D_MODEL, D_MODEL)), kv_spec, kv_spec],
        out_specs=tile_spec,
        out_shape=jax.ShapeDtypeStruct((B, T, D_MODEL), F32),
        compiler_params=_params(("arbitrary", "arbitrary")),
        name="p_attn",
    )(h1_p, gx, wq_b, wo_b, mk_p, mv_p)

    ffn_w_specs = [_const_spec((1, D_MODEL)), _const_spec((D_MODEL, D_FF)), _const_spec((D_MODEL, D_FF)),
                   _const_spec((CONV_W, D_FF)), _const_spec((1, D_FF)), _const_spec((D_FF, D_MODEL)),
                   _const_spec((1, D_MODEL))]
    y_p, tail_p = pl.pallas_call(
        _p_ffn_kernel,
        grid=(B, nt),
        in_specs=[tile_spec] + ffn_w_specs,
        out_specs=[tile_spec, pl.BlockSpec((None, 8, D_FF), lambda b, t: (b, 0, 0))],
        out_shape=[jax.ShapeDtypeStruct((B, T, D_MODEL), F32), jax.ShapeDtypeStruct((B, 8, D_FF), F32)],
        scratch_shapes=[pltpu.VMEM((8, D_FF), F32)],
        compiler_params=_params(("arbitrary", "arbitrary")),
        name="p_ffn",
    )(h2_p, gffn, wg_b, wu_b, cw, cb, wd_b, gfin)

    xs = x_sample.reshape(ntok_s, D_MODEL)
    s_in = state_gla[0]
    tok_spec = pl.BlockSpec((TT, D_MODEL), lambda i: (i, 0))
    st_spec = pl.BlockSpec((rows_per_step, GLA_HEADS, GLA_DK, GLA_DV), lambda i: (i, 0, 0, 0))
    h1_s, q_s, sv_s, s_s = pl.pallas_call(
        _s_mixer_kernel,
        grid=(ntok_s // TT,),
        in_specs=[tok_spec, st_spec] + mixer_w_specs + [
            _const_spec((SGU_HEADS, STOK, STOK)), _const_spec((STOK, SGU_WIDTH)), _const_spec((D_MODEL, D_MODEL)),
            _const_spec((1, D_MODEL)), _const_spec((D_MODEL, D_MODEL))],
        out_specs=[tok_spec, tok_spec, pl.BlockSpec((TT, SGU_WIDTH), lambda i: (i, 0)), st_spec],
        out_shape=[jax.ShapeDtypeStruct((ntok_s, D_MODEL), F32), jax.ShapeDtypeStruct((ntok_s, D_MODEL), F32),
                   jax.ShapeDtypeStruct((ntok_s, SGU_WIDTH), F32),
                   jax.ShapeDtypeStruct((RB, GLA_HEADS, GLA_DK, GLA_DV), F32)],
        compiler_params=_params(("arbitrary",)),
        name="s_mixer",
    )(xs, s_in, gmix, w_in_r, w_alpha_p, b_alpha_r, ggla, gsgu, ws_s, bs_s, w_out_b, gx, wq_b)

    ck = cache_mem_k[0]
    cv = cache_mem_v[0]
    q3_spec = pl.BlockSpec((ATT_ROWS, TS, D_MODEL), lambda i: (i, 0, 0))
    ckv_spec = pl.BlockSpec((ATT_ROWS, N_MEM, MEM_HEADS, MEM_DH), lambda i: (i, 0, 0, 0))
    ao_s = pl.pallas_call(
        _s_attn_kernel,
        grid=(RB // ATT_ROWS,),
        in_specs=[q3_spec, ckv_spec, ckv_spec],
        out_specs=q3_spec,
        out_shape=jax.ShapeDtypeStruct((RB, TS, D_MODEL), F32),
        compiler_params=_params(("arbitrary",)),
        name="s_attn",
    )(q_s.reshape(RB, TS, D_MODEL), ck, cv)

    buf = state_conv[0]
    buf_a = jnp.pad(buf[:, 1:2], ((0, 0), (0, TS - 1), (0, 0))).reshape(ntok_s, D_FF)
    buf_b = jnp.pad(buf, ((0, 0), (0, TS - 2), (0, 0))).reshape(ntok_s, D_FF)
    ff_tok = pl.BlockSpec((TT, D_FF), lambda i: (i, 0))
    tail_spec = pl.BlockSpec((rows_per_step, D_FF), lambda i: (i, 0))
    y_s, ta_s, tb_s = pl.pallas_call(
        _s_ffn_kernel,
        grid=(ntok_s // TT,),
        in_specs=[tok_spec, tok_spec, _const_spec((D_MODEL, D_MODEL)), ff_tok, ff_tok] + ffn_w_specs,
        out_specs=[tok_spec, tail_spec, tail_spec],
        out_shape=[jax.ShapeDtypeStruct((ntok_s, D_MODEL), F32),
                   jax.ShapeDtypeStruct((RB, D_FF), F32), jax.ShapeDtypeStruct((RB, D_FF), F32)],
        compiler_params=_params(("arbitrary",)),
        name="s_ffn",
    )(h1_s, ao_s.reshape(ntok_s, D_MODEL), wo_b, buf_a, buf_b, gffn, wg_b, wu_b, cw, cb, wd_b, gfin)

    return (y_p,
            y_s.reshape(RB, TS, D_MODEL),
            s_p[None],
            tail_p[:, 6:8][None],
            mk_p.reshape(1, B, N_MEM, MEM_HEADS, MEM_DH),
            mv_p.reshape(1, B, N_MEM, MEM_HEADS, MEM_DH),
            s_s[None],
            jnp.stack([ta_s, tb_s], axis=1)[None],
            sv_s.reshape(1, RB, TS, SGU_HEADS, SGU_DH))
```

```python
import functools

import jax
import jax.numpy as jnp
from jax import lax
from jax.experimental import pallas as pl
from jax.experimental.pallas import tpu as pltpu

F32 = jnp.float32
BF16 = jnp.bfloat16

D_MODEL = 1024
GLA_HEADS = 4
GLA_DK = 64
GLA_DV = 128
QK_COLS = GLA_HEADS * GLA_DK
GLA_WIDTH = GLA_HEADS * GLA_DV
GLA_RANK = 16
GLA_TAU = 16.0
SGU_HEADS = 4
SGU_DH = 128
SGU_WIDTH = SGU_HEADS * SGU_DH
SGU_CHUNK = 128
N_MEM = 256
MEM_HEADS = 4
MEM_DH = 256
D_FF = 2816
CONV_W = 3
EPS = 1e-6

LANES = 128
RANK_PAD = LANES
D_IN_PAD = 2 * QK_COLS + 2 * GLA_WIDTH + 2 * SGU_WIDTH + RANK_PAD
TT = 256
TBIG = 1024
GLA_BLOCK = 64
SROWS = 32
STOK = SROWS * 4
FF_CHUNK = 256
ATT_ROWS = 8
VMEM_LIMIT = 56 * 1024 * 1024


def _dot(a, b):
    return jnp.dot(a.astype(BF16), b.astype(BF16), preferred_element_type=F32)


def _dot_nt(a, b):
    return lax.dot_general(a.astype(BF16), b.astype(BF16), (((1,), (1,)), ((), ())),
                           preferred_element_type=F32)


def _dot_tn(a, b):
    return lax.dot_general(a.astype(BF16), b.astype(BF16), (((0,), (0,)), ((), ())),
                           preferred_element_type=F32)


def _split3(x):
    hi = x.astype(BF16)
    r1 = x - hi.astype(F32)
    mid = r1.astype(BF16)
    lo = (r1 - mid.astype(F32)).astype(BF16)
    return hi, mid, lo


def _dot_exact_lhs(sel, parts):
    hi, mid, lo = parts
    return (jnp.dot(sel, hi, preferred_element_type=F32)
            + jnp.dot(sel, mid, preferred_element_type=F32)
            + jnp.dot(sel, lo, preferred_element_type=F32))


def _rms(x, g):
    ms = jnp.mean(x * x, axis=-1, keepdims=True)
    return x * lax.rsqrt(ms + EPS) * g


def _rms_heads(x, g, nh, dh):
    outs = []
    for h in range(nh):
        xh = x[:, h * dh:(h + 1) * dh]
        ms = jnp.mean(xh * xh, axis=-1, keepdims=True)
        outs.append(xh * lax.rsqrt(ms + EPS))
    return jnp.concatenate(outs, axis=-1) * g


def _gelu(x):
    c = 0.7978845608028654
    return x * (0.5 * (1.0 + jnp.tanh(c * (x + 0.044715 * (x * x * x)))))


def _silu(x):
    return x * (1.0 / (1.0 + jnp.exp(-x)))


def _log_sigmoid(x):
    return jnp.minimum(x, 0.0) - jnp.log1p(jnp.exp(-jnp.abs(x)))


def _iota(shape, dim):
    return lax.broadcasted_iota(jnp.int32, shape, dim)


def _in_proj(x, gmix, w_in, w_alpha, b_alpha):
    xn = _rms(x, gmix)
    p = _dot(xn, w_in)
    q = p[:, 0:256] * (GLA_DK ** -0.5)
    k = p[:, 256:512]
    v = p[:, 512:1024]
    r = p[:, 1024:1536]
    u = p[:, 1536:2048]
    sv = p[:, 2048:2560]
    a = p[:, 2560:2688]
    xg = _dot(a, w_alpha) + b_alpha
    logg = _log_sigmoid(xg) * (1.0 / GLA_TAU)
    return q, k, v, r, u, sv, logg


def _mix_out(x, o, r, u, z, ggla, w_out):
    og = _rms_heads(o, ggla, GLA_HEADS, GLA_DV) * _silu(r)
    s_out = u * z
    y = _dot(og, w_out[0:GLA_WIDTH, :]) + _dot(s_out, w_out[GLA_WIDTH:, :])
    return x + y


def _gla_prompt_tile(q, k, v, logg, st_ref):
    n = TT
    ri = _iota((n, n), 0)
    ci = _iota((n, n), 1)
    low = (ci <= ri).astype(BF16)
    bt = _dot_exact_lhs(low, _split3(logg))
    headm = (ri >> 6) == (ci >> 6)
    outs = []
    for blk in range(n // GLA_BLOCK):
        r0 = blk * GLA_BLOCK
        r1 = r0 + GLA_BLOCK
        nn = -(-r1 // LANES) * LANES
        if blk == 0:
            bq = bt[0:GLA_BLOCK]
            ek = jnp.exp(-bt[0:nn])
        else:
            ref = bt[r0 - 1:r0, :]
            bq = bt[r0:r1] - ref
            ek = jnp.exp(ref - bt[0:nn])
        qb = q[r0:r1] * jnp.exp(bq)
        qs = jnp.where(headm, jnp.concatenate([qb] * GLA_HEADS, axis=0), 0.0)
        kb = k[0:nn] * ek
        sc = _dot_nt(qs, kb)
        cm = _iota((n, nn), 1) <= (_iota((n, nn), 0) & (GLA_BLOCK - 1)) + r0
        sc = jnp.where(cm, sc, 0.0)
        ov = _dot(sc, v[0:nn])
        outs.append(jnp.concatenate(
            [ov[h * GLA_BLOCK:(h + 1) * GLA_BLOCK, h * GLA_DV:(h + 1) * GLA_DV] for h in range(GLA_HEADS)],
            axis=1))
    o_intra = jnp.concatenate(outs, axis=0)
    st = st_ref[...]
    o_state = _dot_nt(q * jnp.exp(bt), st)
    bl = bt[n - 1:n, :]
    kh = k * jnp.exp(bl - bt)
    upd = _dot_tn(v, kh)
    bdm = (_iota((GLA_WIDTH, QK_COLS), 0) >> 7) == (_iota((GLA_WIDTH, QK_COLS), 1) >> 6)
    st_ref[...] = jnp.exp(bl) * st + jnp.where(bdm, upd, 0.0)
    return o_intra + o_state


def _p_mixer_kernel(x_ref, gmix_ref, win_ref, walpha_ref, balpha_ref, ggla_ref, gsgu_ref,
                    ws_ref, bs_ref, wout_ref, h_ref, s_ref, st_ref):
    t = pl.program_id(1)

    @pl.when(t == 0)
    def _():
        st_ref[...] = jnp.zeros_like(st_ref)

    x = x_ref[...]
    q, k, v, r, u, sv, logg = _in_proj(x, gmix_ref[...], win_ref[...], walpha_ref[...], balpha_ref[...])
    o = _gla_prompt_tile(q, k, v, logg, st_ref)
    u = _gelu(u)
    svn = _rms_heads(_gelu(sv), gsgu_ref[...], SGU_HEADS, SGU_DH)
    ri = _iota((TT, TT), 0)
    ci = _iota((TT, TT), 1)
    wm = ((ri >> 7) == (ci >> 7)) & (ci <= ri)
    zs = []
    for h in range(SGU_HEADS):
        w = jnp.where(wm, ws_ref[h].astype(F32), 0.0)
        zs.append(_dot(w, svn[:, h * SGU_DH:(h + 1) * SGU_DH]))
    z = jnp.concatenate(zs, axis=1) + bs_ref[...]
    h_ref[...] = _mix_out(x, o, r, u, z, ggla_ref[...], wout_ref)

    @pl.when(t == pl.num_programs(1) - 1)
    def _():
        s_full = st_ref[...].T
        for h in range(GLA_HEADS):
            s_ref[h] = s_full[h * GLA_DK:(h + 1) * GLA_DK, h * GLA_DV:(h + 1) * GLA_DV]


def _gla_sample_block(q, k, v, logg, sin_ref, sout_ref, row0):
    n = STOK
    ri = _iota((n, n), 0)
    ci = _iota((n, n), 1)
    same = (ri >> 2) == (ci >> 2)
    causal = same & (ci <= ri)
    parts = _split3(logg)
    b = _dot_exact_lhs(causal.astype(BF16), parts)
    bl = _dot_exact_lhs(same.astype(BF16), parts)
    qt = q * jnp.exp(b)
    kt = k * jnp.exp(-b)
    kh = k * jnp.exp(bl - b)
    headm = (_iota((GLA_HEADS * n, QK_COLS), 0) >> 7) == (_iota((GLA_HEADS * n, QK_COLS), 1) >> 6)
    qs = jnp.where(headm, jnp.concatenate([qt] * GLA_HEADS, axis=0), 0.0)
    sc = _dot_nt(qs, kt)
    tok = _iota((GLA_HEADS * n, n), 0) & (n - 1)
    col = _iota((GLA_HEADS * n, n), 1)
    sc = jnp.where(((tok >> 2) == (col >> 2)) & (col <= tok), sc, 0.0)
    ov = _dot(sc, v)
    o_intra = jnp.concatenate(
        [ov[h * n:(h + 1) * n, h * GLA_DV:(h + 1) * GLA_DV] for h in range(GLA_HEADS)], axis=1)

    nexp = SROWS * GLA_DK
    expand = ((_iota((GLA_DK, nexp), 1) & (GLA_DK - 1)) == _iota((GLA_DK, nexp), 0)).astype(BF16)
    expand_t = ((_iota((nexp, GLA_DK), 0) & (GLA_DK - 1)) == _iota((nexp, GLA_DK), 1)).astype(BF16)
    bd = (_iota((n, nexp), 0) >> 2) == (_iota((n, nexp), 1) >> 6)
    bd_t = (_iota((nexp, n), 0) >> 6) == (_iota((nexp, n), 1) >> 2)
    last_tok = (_iota((n, QK_COLS), 0) & 3) == 3
    dec_t = jnp.where(last_tok, jnp.exp(bl), 0.0).T
    kh_t = kh.T
    ones = jnp.ones((n, GLA_DV), BF16)
    outs = []
    for h in range(GLA_HEADS):
        dsl = slice(h * GLA_DK, (h + 1) * GLA_DK)
        s_in = sin_ref[row0:row0 + SROWS, h].reshape(nexp, GLA_DV)
        qe = jnp.where(bd, _dot(qt[:, dsl], expand), 0.0)
        outs.append(_dot(qe, s_in))
        ke_t = jnp.where(bd_t, _dot(expand_t, kh_t[dsl, :]), 0.0)
        upd = _dot(ke_t, v[:, h * GLA_DV:(h + 1) * GLA_DV])
        dcol = 0.0
        for piece in _split3(dec_t[dsl, :]):
            de_t = jnp.where(bd_t, jnp.dot(expand_t, piece, preferred_element_type=F32), 0.0)
            dcol = dcol + jnp.dot(de_t.astype(BF16), ones, preferred_element_type=F32)
        s_new = dcol * s_in + upd
        sout_ref[row0:row0 + SROWS, h] = s_new.reshape(SROWS, GLA_DK, GLA_DV)
    return o_intra + jnp.concatenate(outs, axis=1)


def _s_mixer_kernel(x_ref, sin_ref, gmix_ref, win_ref, walpha_ref, balpha_ref, ggla_ref, gsgu_ref,
                    ws_ref, bs_ref, wout_ref, gx_ref, wq_ref,
                    h_ref, q_ref, sv_ref, sout_ref):
    x = x_ref[...]
    q, k, v, r, u, sv, logg = _in_proj(x, gmix_ref[...], win_ref[...], walpha_ref[...], balpha_ref[...])
    u = _gelu(u)
    svn = _rms_heads(_gelu(sv), gsgu_ref[...], SGU_HEADS, SGU_DH)
    sv_ref[...] = svn
    ri = _iota((STOK, STOK), 0)
    ci = _iota((STOK, STOK), 1)
    wm = ((ri >> 2) == (ci >> 2)) & (ci <= ri)
    os_, zs = [], []
    for sb in range(TT // STOK):
        ts = slice(sb * STOK, (sb + 1) * STOK)
        os_.append(_gla_sample_block(q[ts], k[ts], v[ts], logg[ts], sin_ref, sout_ref, sb * SROWS))
        zh = []
        for h in range(SGU_HEADS):
            w = jnp.where(wm, ws_ref[h].astype(F32), 0.0)
            zh.append(_dot(w, svn[ts, h * SGU_DH:(h + 1) * SGU_DH]))
        zs.append(jnp.concatenate(zh, axis=1) + bs_ref[...])
    o = jnp.concatenate(os_, axis=0)
    z = jnp.concatenate(zs, axis=0)
    h1 = _mix_out(x, o, r, u, z, ggla_ref[...], wout_ref)
    h_ref[...] = h1
    q_ref[...] = _dot(_rms(h1, gx_ref[...]), wq_ref[...])


def _softmax_rows(s):
    m = jnp.max(s, axis=-1, keepdims=True)
    e = jnp.exp(s - m)
    return e / jnp.sum(e, axis=-1, keepdims=True)


def _memkv_kernel(mem_ref, gmem_ref, wk_ref, wv_ref, mk_ref, mv_ref):
    mn = _rms(mem_ref[...], gmem_ref[...])
    mk_ref[...] = _dot(mn, wk_ref[...])
    mv_ref[...] = _dot(mn, wv_ref[...])


def _p_attn_kernel(h_ref, gx_ref, wq_ref, wo_ref, mk_ref, mv_ref, o_ref):
    h1 = h_ref[...]
    q = _dot(_rms(h1, gx_ref[...]), wq_ref[...])
    outs = []
    for h in range(MEM_HEADS):
        sl = slice(h * MEM_DH, (h + 1) * MEM_DH)
        s = _dot_nt(q[:, sl], mk_ref[:, sl]) * (MEM_DH ** -0.5)
        outs.append(_dot(_softmax_rows(s), mv_ref[:, sl]))
    o_ref[...] = h1 + _dot(jnp.concatenate(outs, axis=1), wo_ref[...])


def _s_attn_kernel(q_ref, k_ref, v_ref, o_ref):
    nq = MEM_HEADS * 4
    nk = N_MEM * MEM_HEADS
    own_head = (_iota((nq, nk), 1) & (MEM_HEADS - 1)) == (_iota((nq, nk), 0) >> 2)
    for r in range(ATT_ROWS):
        q = q_ref[r]
        qh = jnp.concatenate([q[:, h * MEM_DH:(h + 1) * MEM_DH] for h in range(MEM_HEADS)], axis=0)
        kk = k_ref[r].reshape(nk, MEM_DH)
        vv = v_ref[r].reshape(nk, MEM_DH)
        s = jnp.where(own_head, _dot_nt(qh, kk) * (MEM_DH ** -0.5), -jnp.inf)
        o = _dot(_softmax_rows(s), vv)
        for h in range(MEM_HEADS):
            o_ref[r, :, h * MEM_DH:(h + 1) * MEM_DH] = o[4 * h:4 * h + 4]


def _ffn_chunks(hn, wg_ref, wu_ref, cw_ref, cb_ref, wd_ref, prod_ref, shifts, sink):
    hb = hn.astype(BF16)
    for c in range(D_FF // FF_CHUNK):
        cs = slice(c * FF_CHUNK, (c + 1) * FF_CHUNK)
        g = jnp.dot(hb, wg_ref[:, cs], preferred_element_type=F32)
        s1, s2 = shifts(g, cs)
        conv = cb_ref[:, cs] + ((cw_ref[0:1, cs] * s2 + cw_ref[1:2, cs] * s1) + cw_ref[2:3, cs] * g)
        up = jnp.dot(hb, wu_ref[:, cs], preferred_element_type=F32)
        prod_ref[:, cs] = (_gelu(conv) * up).astype(BF16)
        sink(g, cs)
    return jnp.dot(prod_ref[...], wd_ref[...], preferred_element_type=F32)


def _p_ffn_kernel(h_ref, gffn_ref, wg_ref, wu_ref, cw_ref, cb_ref, wd_ref, gfin_ref,
                  y_ref, tail_ref, carry_ref, prod_ref):
    t = pl.program_id(1)

    @pl.when(t == 0)
    def _():
        carry_ref[...] = jnp.zeros_like(carry_ref)

    h2 = h_ref[...]
    m = h2.shape[0]
    row = _iota((m, FF_CHUNK), 0)

    def shifts(g, cs):
        prev = carry_ref[:, cs]
        p1 = prev[7:8, :]
        p2 = prev[6:7, :]
        s1 = jnp.where(row == 0, p1, pltpu.roll(g, 1, 0))
        s2 = jnp.where(row == 0, p2, jnp.where(row == 1, p1, pltpu.roll(g, 2, 0)))
        return s1, s2

    def sink(g, cs):
        carry_ref[:, cs] = g[m - 8:m, :]

    f = _ffn_chunks(_rms(h2, gffn_ref[...]), wg_ref, wu_ref, cw_ref, cb_ref, wd_ref, prod_ref, shifts, sink)
    y_ref[...] = _rms(h2 + f, gfin_ref[...])

    @pl.when(t == pl.num_programs(1) - 1)
    def _():
        tail_ref[...] = carry_ref[...]


def _s_ffn_kernel(h_ref, ao_ref, wo_ref, bufa_ref, bufb_ref, gffn_ref, wg_ref, wu_ref, cw_ref, cb_ref,
                  wd_ref, gfin_ref, y_ref, ta_ref, tb_ref, prod_ref):
    h2 = h_ref[...] + _dot(ao_ref[...], wo_ref[...])
    tpos = _iota((TT, FF_CHUNK), 0) & 3

    def shifts(g, cs):
        s1 = jnp.where(tpos == 0, bufa_ref[:, cs], pltpu.roll(g, 1, 0))
        s2 = jnp.where(tpos < 2, bufb_ref[:, cs], pltpu.roll(g, 2, 0))
        return s1, s2

    srow = _iota((TT // 4, TT), 0) * 4
    scol = _iota((TT // 4, TT), 1)
    sel_a = (scol == srow + 2).astype(BF16)
    sel_b = (scol == srow + 3).astype(BF16)

    def sink(g, cs):
        parts = _split3(g)
        ta_ref[:, cs] = _dot_exact_lhs(sel_a, parts)
        tb_ref[:, cs] = _dot_exact_lhs(sel_b, parts)

    f = _ffn_chunks(_rms(h2, gffn_ref[...]), wg_ref, wu_ref, cw_ref, cb_ref, wd_ref, prod_ref, shifts, sink)
    y_ref[...] = _rms(h2 + f, gfin_ref[...])


def _const_spec(shape, single=True):
    nd = len(shape)
    kw = {"pipeline_mode": pl.Buffered(1)} if single else {}
    return pl.BlockSpec(shape, lambda *_: (0,) * nd, **kw)


def _params(sem):
    return pltpu.CompilerParams(dimension_semantics=sem, vmem_limit_bytes=VMEM_LIMIT)


def kernel(x_prompt, x_sample, mem_prompt, state_gla, state_conv, cache_mem_k, cache_mem_v, g_mix, w_in, w_alpha, b_alpha, g_gla_out, g_sgu, w_s, b_s, w_out, g_x, g_mem, wq_x, wk_x, wv_x, wo_x, g_ffn, w_gate, w_up, conv_w, conv_b, w_down, g_final):
    B, T, _ = x_prompt.shape
    RB, TS, _ = x_sample.shape
    assert T % TT == 0 and T % TBIG == 0 and TS == 4 and (RB * TS) % TT == 0 and RB % ATT_ROWS == 0
    nt = T // TT
    ntok_s = RB * TS
    rows_per_step = TT // TS

    w_in0 = w_in[0]
    cut = 2 * QK_COLS + 2 * GLA_WIDTH
    w_in_r = jnp.concatenate(
        [w_in0[:, :cut], w_in0[:, cut + GLA_RANK:], w_in0[:, cut:cut + GLA_RANK],
         jnp.zeros((D_MODEL, RANK_PAD - GLA_RANK), F32)], axis=1).astype(BF16)
    w_alpha_p = jnp.concatenate([w_alpha[0], jnp.zeros((RANK_PAD - GLA_RANK, QK_COLS), F32)], axis=0).astype(BF16)
    b_alpha_r = b_alpha[0].reshape(1, QK_COLS)
    gmix = g_mix[0].reshape(1, D_MODEL)
    ggla = g_gla_out[0].reshape(1, GLA_WIDTH)
    gsgu = g_sgu[0].reshape(1, SGU_WIDTH)
    gx = g_x[0].reshape(1, D_MODEL)
    gmem = g_mem[0].reshape(1, D_MODEL)
    gffn = g_ffn[0].reshape(1, D_MODEL)
    gfin = g_final.reshape(1, D_MODEL)
    w_out_b = w_out[0].astype(BF16)
    wq_b, wk_b, wv_b, wo_b = (w[0].astype(BF16) for w in (wq_x, wk_x, wv_x, wo_x))
    wg_b, wu_b, wd_b = (w[0].astype(BF16) for w in (w_gate, w_up, w_down))
    cw = conv_w[0]
    cb = conv_b[0].reshape(1, D_FF)
    ws_p = jnp.tile(w_s[0], (1, TT // SGU_CHUNK, TT // SGU_CHUNK)).astype(BF16)
    bs_p = jnp.tile(jnp.repeat(b_s[0].T, SGU_DH, axis=1), (TT // SGU_CHUNK, 1))
    ws_s = jnp.tile(w_s[0][:, :TS, :TS], (1, SROWS, SROWS)).astype(BF16)
    bs_s = jnp.tile(jnp.repeat(b_s[0][:, :TS].T, SGU_DH, axis=1), (SROWS, 1))

    mixer_w_specs = [
        _const_spec((1, D_MODEL)), _const_spec((D_MODEL, D_IN_PAD)), _const_spec((RANK_PAD, QK_COLS)),
        _const_spec((1, QK_COLS)), _const_spec((1, GLA_WIDTH)), _const_spec((1, SGU_WIDTH)),
    ]

    mk_p, mv_p = pl.pallas_call(
        _memkv_kernel,
        grid=(B,),
        in_specs=[pl.BlockSpec((None, N_MEM, D_MODEL), lambda b: (b, 0, 0)),
                  _const_spec((1, D_MODEL)), _const_spec((D_MODEL, D_MODEL)), _const_spec((D_MODEL, D_MODEL))],
        out_specs=[pl.BlockSpec((None, N_MEM, D_MODEL), lambda b: (b, 0, 0))] * 2,
        out_shape=[jax.ShapeDtypeStruct((B, N_MEM, D_MODEL), F32)] * 2,
        compiler_params=_params(("arbitrary",)),
        name="memkv",
    )(mem_prompt, gmem, wk_b, wv_b)

    tile_spec = pl.BlockSpec((None, TT, D_MODEL), lambda b, t: (b, t, 0))
    h1_p, s_p = pl.pallas_call(
        _p_mixer_kernel,
        grid=(B, nt),
        in_specs=[tile_spec] + mixer_w_specs + [
            _const_spec((SGU_HEADS, TT, TT)), _const_spec((TT, SGU_WIDTH)), _const_spec((D_MODEL, D_MODEL))],
        out_specs=[tile_spec,
                   pl.BlockSpec((None, GLA_HEADS, GLA_DK, GLA_DV), lambda b, t: (b, 0, 0, 0))],
        out_shape=[jax.ShapeDtypeStruct((B, T, D_MODEL), F32),
                   jax.ShapeDtypeStruct((B, GLA_HEADS, GLA_DK, GLA_DV), F32)],
        scratch_shapes=[pltpu.VMEM((GLA_WIDTH, QK_COLS), F32)],
        compiler_params=_params(("arbitrary", "arbitrary")),
        name="p_mixer",
    )(x_prompt, gmix, w_in_r, w_alpha_p, b_alpha_r, ggla, gsgu, ws_p, bs_p, w_out_b)

    kv_spec = pl.BlockSpec((None, N_MEM, D_MODEL), lambda b, t: (b, 0, 0))
    big_spec = pl.BlockSpec((None, TBIG, D_MODEL), lambda b, t: (b, t, 0))
    h2_p = pl.pallas_call(
        _p_attn_kernel,
        grid=(B, T // TBIG),
        in_specs=[big_spec, _const_spec((1, D_MODEL)), _const_spec((D_MODEL, D_MODEL)),
                  _const_spec((D_MODEL, D_MODEL)), kv_spec, kv_spec],
        out_specs=big_spec,
        out_shape=jax.ShapeDtypeStruct((B, T, D_MODEL), F32),
        compiler_params=_params(("arbitrary", "arbitrary")),
        name="p_attn",
    )(h1_p, gx, wq_b, wo_b, mk_p, mv_p)

    ffn_w_specs = [_const_spec((1, D_MODEL)), _const_spec((D_MODEL, D_FF)), _const_spec((D_MODEL, D_FF)),
                   _const_spec((CONV_W, D_FF)), _const_spec((1, D_FF)), _const_spec((D_FF, D_MODEL)),
                   _const_spec((1, D_MODEL))]
    y_p, tail_p = pl.pallas_call(
        _p_ffn_kernel,
        grid=(B, T // TBIG),
        in_specs=[big_spec] + ffn_w_specs,
        out_specs=[big_spec, pl.BlockSpec((None, 8, D_FF), lambda b, t: (b, 0, 0))],
        out_shape=[jax.ShapeDtypeStruct((B, T, D_MODEL), F32), jax.ShapeDtypeStruct((B, 8, D_FF), F32)],
        scratch_shapes=[pltpu.VMEM((8, D_FF), F32), pltpu.VMEM((TBIG, D_FF), BF16)],
        compiler_params=_params(("arbitrary", "arbitrary")),
        name="p_ffn",
    )(h2_p, gffn, wg_b, wu_b, cw, cb, wd_b, gfin)

    xs = x_sample.reshape(ntok_s, D_MODEL)
    s_in = state_gla[0]
    tok_spec = pl.BlockSpec((TT, D_MODEL), lambda i: (i, 0))
    st_spec = pl.BlockSpec((rows_per_step, GLA_HEADS, GLA_DK, GLA_DV), lambda i: (i, 0, 0, 0))
    h1_s, q_s, sv_s, s_s = pl.pallas_call(
        _s_mixer_kernel,
        grid=(ntok_s // TT,),
        in_specs=[tok_spec, st_spec] + mixer_w_specs + [
            _const_spec((SGU_HEADS, STOK, STOK)), _const_spec((STOK, SGU_WIDTH)), _const_spec((D_MODEL, D_MODEL)),
            _const_spec((1, D_MODEL)), _const_spec((D_MODEL, D_MODEL))],
        out_specs=[tok_spec, tok_spec, pl.BlockSpec((TT, SGU_WIDTH), lambda i: (i, 0)), st_spec],
        out_shape=[jax.ShapeDtypeStruct((ntok_s, D_MODEL), F32), jax.ShapeDtypeStruct((ntok_s, D_MODEL), F32),
                   jax.ShapeDtypeStruct((ntok_s, SGU_WIDTH), F32),
                   jax.ShapeDtypeStruct((RB, GLA_HEADS, GLA_DK, GLA_DV), F32)],
        compiler_params=_params(("arbitrary",)),
        name="s_mixer",
    )(xs, s_in, gmix, w_in_r, w_alpha_p, b_alpha_r, ggla, gsgu, ws_s, bs_s, w_out_b, gx, wq_b)

    ck = cache_mem_k[0]
    cv = cache_mem_v[0]
    q3_spec = pl.BlockSpec((ATT_ROWS, TS, D_MODEL), lambda i: (i, 0, 0))
    ckv_spec = pl.BlockSpec((ATT_ROWS, N_MEM, MEM_HEADS, MEM_DH), lambda i: (i, 0, 0, 0))
    ao_s = pl.pallas_call(
        _s_attn_kernel,
        grid=(RB // ATT_ROWS,),
        in_specs=[q3_spec, ckv_spec, ckv_spec],
        out_specs=q3_spec,
        out_shape=jax.ShapeDtypeStruct((RB, TS, D_MODEL), F32),
        compiler_params=_params(("arbitrary",)),
        name="s_attn",
    )(q_s.reshape(RB, TS, D_MODEL), ck, cv)

    buf = state_conv[0]
    buf_a = jnp.pad(buf[:, 1:2], ((0, 0), (0, TS - 1), (0, 0))).reshape(ntok_s, D_FF)
    buf_b = jnp.pad(buf, ((0, 0), (0, TS - 2), (0, 0))).reshape(ntok_s, D_FF)
    ff_tok = pl.BlockSpec((TT, D_FF), lambda i: (i, 0))
    tail_spec = pl.BlockSpec((rows_per_step, D_FF), lambda i: (i, 0))
    y_s, ta_s, tb_s = pl.pallas_call(
        _s_ffn_kernel,
        grid=(ntok_s // TT,),
        in_specs=[tok_spec, tok_spec, _const_spec((D_MODEL, D_MODEL)), ff_tok, ff_tok] + ffn_w_specs,
        out_specs=[tok_spec, tail_spec, tail_spec],
        out_shape=[jax.ShapeDtypeStruct((ntok_s, D_MODEL), F32),
                   jax.ShapeDtypeStruct((RB, D_FF), F32), jax.ShapeDtypeStruct((RB, D_FF), F32)],
        scratch_shapes=[pltpu.VMEM((TT, D_FF), BF16)],
        compiler_params=_params(("arbitrary",)),
        name="s_ffn",
    )(h1_s, ao_s.reshape(ntok_s, D_MODEL), wo_b, buf_a, buf_b, gffn, wg_b, wu_b, cw, cb, wd_b, gfin)

    return (y_p,
            y_s.reshape(RB, TS, D_MODEL),
            s_p[None],
            tail_p[:, 6:8][None],
            mk_p.reshape(1, B, N_MEM, MEM_HEADS, MEM_DH),
            mv_p.reshape(1, B, N_MEM, MEM_HEADS, MEM_DH),
            s_s[None],
            jnp.stack([ta_s, tb_s], axis=1)[None],
            sv_s.reshape(1, RB, TS, SGU_HEADS, SGU_DH))
```

```python
import functools

import jax
import jax.numpy as jnp
from jax import lax
from jax.experimental import pallas as pl
from jax.experimental.pallas import tpu as pltpu

F32 = jnp.float32
BF16 = jnp.bfloat16

D_MODEL = 1024
GLA_HEADS = 4
GLA_DK = 64
GLA_DV = 128
QK_COLS = GLA_HEADS * GLA_DK
GLA_WIDTH = GLA_HEADS * GLA_DV
GLA_RANK = 16
GLA_TAU = 16.0
SGU_HEADS = 4
SGU_DH = 128
SGU_WIDTH = SGU_HEADS * SGU_DH
SGU_CHUNK = 128
N_MEM = 256
MEM_HEADS = 4
MEM_DH = 256
D_FF = 2816
CONV_W = 3
EPS = 1e-6

LANES = 128
RANK_PAD = LANES
TT = 256
TBIG = 1024
TMIX = 1024
GLA_BLOCK = 64
SROWS = 32
STOK = SROWS * 4
FF_CHUNK = 256
ATT_ROWS = 8
VMEM_LIMIT = 56 * 1024 * 1024


def _dot(a, b):
    return jnp.dot(a.astype(BF16), b.astype(BF16), preferred_element_type=F32)


def _dot_nt(a, b):
    return lax.dot_general(a.astype(BF16), b.astype(BF16), (((1,), (1,)), ((), ())),
                           preferred_element_type=F32)


def _dot_tn(a, b):
    return lax.dot_general(a.astype(BF16), b.astype(BF16), (((0,), (0,)), ((), ())),
                           preferred_element_type=F32)


def _split3(x):
    hi = x.astype(BF16)
    r1 = x - hi.astype(F32)
    mid = r1.astype(BF16)
    lo = (r1 - mid.astype(F32)).astype(BF16)
    return hi, mid, lo


def _dot_exact_lhs(sel, parts):
    hi, mid, lo = parts
    return (jnp.dot(sel, hi, preferred_element_type=F32)
            + jnp.dot(sel, mid, preferred_element_type=F32)
            + jnp.dot(sel, lo, preferred_element_type=F32))


def _rms(x, g):
    ms = jnp.mean(x * x, axis=-1, keepdims=True)
    return x * lax.rsqrt(ms + EPS) * g


def _rms_heads(x, g, nh, dh):
    outs = []
    for h in range(nh):
        xh = x[:, h * dh:(h + 1) * dh]
        ms = jnp.mean(xh * xh, axis=-1, keepdims=True)
        outs.append(xh * lax.rsqrt(ms + EPS))
    return jnp.concatenate(outs, axis=-1) * g


def _gelu(x):
    c = 0.7978845608028654
    return x * (0.5 * (1.0 + jnp.tanh(c * (x + 0.044715 * (x * x * x)))))


def _silu(x):
    return x * (1.0 / (1.0 + jnp.exp(-x)))


def _log_sigmoid(x):
    return jnp.minimum(x, 0.0) - jnp.log1p(jnp.exp(-jnp.abs(x)))


def _iota(shape, dim):
    return lax.broadcasted_iota(jnp.int32, shape, dim)


def _in_proj(x, gmix, w_qkvr, w_usv, w_a, w_alpha, b_alpha):
    xn = _rms(x, gmix).astype(BF16)
    p = jnp.dot(xn, w_qkvr, preferred_element_type=F32)
    q = p[:, 0:256] * (GLA_DK ** -0.5)
    k = p[:, 256:512]
    v = p[:, 512:1024]
    r = p[:, 1024:1536]
    a = jnp.dot(xn, w_a, preferred_element_type=F32)
    xg = _dot(a, w_alpha) + b_alpha
    logg = _log_sigmoid(xg) * (1.0 / GLA_TAU)
    p2 = jnp.dot(xn, w_usv, preferred_element_type=F32)
    u = p2[:, 0:512]
    sv = p2[:, 512:1024]
    return q, k, v, r, u, sv, logg


def _mix_out(x, o, r, u, z, ggla, w_out):
    og = _rms_heads(o, ggla, GLA_HEADS, GLA_DV) * _silu(r)
    s_out = u * z
    y = _dot(og, w_out[0:GLA_WIDTH, :]) + _dot(s_out, w_out[GLA_WIDTH:, :])
    return x + y


def _gla_prompt_tile(q, k, v, logg, st_ref):
    n = TT
    ri = _iota((n, n), 0)
    ci = _iota((n, n), 1)
    low = (ci <= ri).astype(BF16)
    bt = _dot_exact_lhs(low, _split3(logg))
    headm = (ri >> 6) == (ci >> 6)
    outs = []
    for blk in range(n // GLA_BLOCK):
        r0 = blk * GLA_BLOCK
        r1 = r0 + GLA_BLOCK
        nn = -(-r1 // LANES) * LANES
        if blk == 0:
            bq = bt[0:GLA_BLOCK]
            ek = jnp.exp(-bt[0:nn])
        else:
            ref = bt[r0 - 1:r0, :]
            bq = bt[r0:r1] - ref
            ek = jnp.exp(ref - bt[0:nn])
        qb = q[r0:r1] * jnp.exp(bq)
        qs = jnp.where(headm, jnp.concatenate([qb] * GLA_HEADS, axis=0), 0.0)
        kb = k[0:nn] * ek
        sc = _dot_nt(qs, kb)
        cm = _iota((n, nn), 1) <= (_iota((n, nn), 0) & (GLA_BLOCK - 1)) + r0
        sc = jnp.where(cm, sc, 0.0)
        ov = _dot(sc, v[0:nn])
        outs.append(jnp.concatenate(
            [ov[h * GLA_BLOCK:(h + 1) * GLA_BLOCK, h * GLA_DV:(h + 1) * GLA_DV] for h in range(GLA_HEADS)],
            axis=1))
    o_intra = jnp.concatenate(outs, axis=0)
    st = st_ref[...]
    o_state = _dot_nt(q * jnp.exp(bt), st)
    bl = bt[n - 1:n, :]
    kh = k * jnp.exp(bl - bt)
    upd = _dot_tn(v, kh)
    bdm = (_iota((GLA_WIDTH, QK_COLS), 0) >> 7) == (_iota((GLA_WIDTH, QK_COLS), 1) >> 6)
    st_ref[...] = jnp.exp(bl) * st + jnp.where(bdm, upd, 0.0)
    return o_intra + o_state


def _p_mixer_kernel(x_ref, gmix_ref, wqkvr_ref, wusv_ref, wa_ref, walpha_ref, balpha_ref, ggla_ref, gsgu_ref,
                    ws_ref, bs_ref, wout_ref, h_ref, s_ref, st_ref):
    t = pl.program_id(1)

    @pl.when(t == 0)
    def _():
        st_ref[...] = jnp.zeros_like(st_ref)

    x = x_ref[...]
    q, k, v, r, u, sv, logg = _in_proj(x, gmix_ref[...], wqkvr_ref[...], wusv_ref[...], wa_ref[...],
                                       walpha_ref[...], balpha_ref[...])
    u = _gelu(u)
    svn = _rms_heads(_gelu(sv), gsgu_ref[...], SGU_HEADS, SGU_DH)
    ri = _iota((TT, TT), 0)
    ci = _iota((TT, TT), 1)
    wm = ((ri >> 7) == (ci >> 7)) & (ci <= ri)
    ws = [jnp.where(wm, ws_ref[h].astype(F32), 0.0).astype(BF16) for h in range(SGU_HEADS)]
    os_, zs = [], []
    for sub in range(x.shape[0] // TT):
        ts = slice(sub * TT, (sub + 1) * TT)
        os_.append(_gla_prompt_tile(q[ts], k[ts], v[ts], logg[ts], st_ref))
        zh = [_dot(ws[h], svn[ts, h * SGU_DH:(h + 1) * SGU_DH]) for h in range(SGU_HEADS)]
        zs.append(jnp.concatenate(zh, axis=1) + bs_ref[...])
    o = jnp.concatenate(os_, axis=0)
    z = jnp.concatenate(zs, axis=0)
    h_ref[...] = _mix_out(x, o, r, u, z, ggla_ref[...], wout_ref)

    @pl.when(t == pl.num_programs(1) - 1)
    def _():
        s_full = st_ref[...].T
        for h in range(GLA_HEADS):
            s_ref[h] = s_full[h * GLA_DK:(h + 1) * GLA_DK, h * GLA_DV:(h + 1) * GLA_DV]


def _gla_sample_block(q, k, v, logg, sin_ref, sout_ref, row0):
    n = STOK
    ri = _iota((n, n), 0)
    ci = _iota((n, n), 1)
    same = (ri >> 2) == (ci >> 2)
    causal = same & (ci <= ri)
    parts = _split3(logg)
    b = _dot_exact_lhs(causal.astype(BF16), parts)
    bl = _dot_exact_lhs(same.astype(BF16), parts)
    qt = q * jnp.exp(b)
    kt = k * jnp.exp(-b)
    kh = k * jnp.exp(bl - b)
    headm = (_iota((GLA_HEADS * n, QK_COLS), 0) >> 7) == (_iota((GLA_HEADS * n, QK_COLS), 1) >> 6)
    qs = jnp.where(headm, jnp.concatenate([qt] * GLA_HEADS, axis=0), 0.0)
    sc = _dot_nt(qs, kt)
    tok = _iota((GLA_HEADS * n, n), 0) & (n - 1)
    col = _iota((GLA_HEADS * n, n), 1)
    sc = jnp.where(((tok >> 2) == (col >> 2)) & (col <= tok), sc, 0.0)
    ov = _dot(sc, v)
    o_intra = jnp.concatenate(
        [ov[h * n:(h + 1) * n, h * GLA_DV:(h + 1) * GLA_DV] for h in range(GLA_HEADS)], axis=1)

    nexp = SROWS * GLA_DK
    expand = ((_iota((GLA_DK, nexp), 1) & (GLA_DK - 1)) == _iota((GLA_DK, nexp), 0)).astype(BF16)
    expand_t = ((_iota((nexp, GLA_DK), 0) & (GLA_DK - 1)) == _iota((nexp, GLA_DK), 1)).astype(BF16)
    bd = (_iota((n, nexp), 0) >> 2) == (_iota((n, nexp), 1) >> 6)
    bd_t = (_iota((nexp, n), 0) >> 6) == (_iota((nexp, n), 1) >> 2)
    last_tok = (_iota((n, QK_COLS), 0) & 3) == 3
    dec_t = jnp.where(last_tok, jnp.exp(bl), 0.0).T
    kh_t = kh.T
    ones = jnp.ones((n, GLA_DV), BF16)
    outs = []
    for h in range(GLA_HEADS):
        dsl = slice(h * GLA_DK, (h + 1) * GLA_DK)
        s_in = sin_ref[row0:row0 + SROWS, h].reshape(nexp, GLA_DV)
        qe = jnp.where(bd, _dot(qt[:, dsl], expand), 0.0)
        outs.append(_dot(qe, s_in))
        ke_t = jnp.where(bd_t, _dot(expand_t, kh_t[dsl, :]), 0.0)
        upd = _dot(ke_t, v[:, h * GLA_DV:(h + 1) * GLA_DV])
        dcol = 0.0
        for piece in _split3(dec_t[dsl, :]):
            de_t = jnp.where(bd_t, jnp.dot(expand_t, piece, preferred_element_type=F32), 0.0)
            dcol = dcol + jnp.dot(de_t.astype(BF16), ones, preferred_element_type=F32)
        s_new = dcol * s_in + upd
        sout_ref[row0:row0 + SROWS, h] = s_new.reshape(SROWS, GLA_DK, GLA_DV)
    return o_intra + jnp.concatenate(outs, axis=1)


def _s_mixer_kernel(x_ref, sin_ref, gmix_ref, wqkvr_ref, wusv_ref, wa_ref, walpha_ref, balpha_ref, ggla_ref,
                    gsgu_ref, wsp_ref, bsp_ref, wout_ref, gx_ref, wq_ref,
                    h_ref, q_ref, sv_ref, sout_ref):
    x = x_ref[...].reshape(TT, D_MODEL)
    q, k, v, r, u, sv, logg = _in_proj(x, gmix_ref[...], wqkvr_ref[...], wusv_ref[...], wa_ref[...],
                                       walpha_ref[...], balpha_ref[...])
    u = _gelu(u)
    svn = _rms_heads(_gelu(sv), gsgu_ref[...], SGU_HEADS, SGU_DH)
    sv_ref[...] = svn.reshape(TT // 4, 4, SGU_HEADS, SGU_DH)
    z = jnp.tile(bsp_ref[...], (TT // 8, 1))
    for kk in range(4):
        shifted = svn if kk == 0 else pltpu.roll(svn, kk, 0)
        z = z + jnp.tile(wsp_ref[kk], (TT // 8, 1)) * shifted
    os_ = []
    for sb in range(TT // STOK):
        ts = slice(sb * STOK, (sb + 1) * STOK)
        os_.append(_gla_sample_block(q[ts], k[ts], v[ts], logg[ts], sin_ref, sout_ref, sb * SROWS))
    o = jnp.concatenate(os_, axis=0)
    h1 = _mix_out(x, o, r, u, z, ggla_ref[...], wout_ref)
    h_ref[...] = h1
    q_ref[...] = _dot(_rms(h1, gx_ref[...]), wq_ref[...])


def _softmax_rows(s):
    m = jnp.max(s, axis=-1, keepdims=True)
    e = jnp.exp(s - m)
    return e / jnp.sum(e, axis=-1, keepdims=True)


def _memkv_kernel(mem_ref, gmem_ref, wk_ref, wv_ref, mk_ref, mv_ref, mkb_ref, mvb_ref):
    mn = _rms(mem_ref[...], gmem_ref[...])
    mk = _dot(mn, wk_ref[...])
    mv = _dot(mn, wv_ref[...])
    mk_ref[...] = mk.reshape(N_MEM, MEM_HEADS, MEM_DH)
    mv_ref[...] = mv.reshape(N_MEM, MEM_HEADS, MEM_DH)
    mkb_ref[...] = mk.astype(BF16)
    mvb_ref[...] = mv.astype(BF16)


def _p_attn_kernel(h_ref, gx_ref, wq_ref, wo_ref, mk_ref, mv_ref, o_ref):
    h1 = h_ref[...]
    q = _dot(_rms(h1, gx_ref[...]), wq_ref[...])
    outs = []
    for h in range(MEM_HEADS):
        sl = slice(h * MEM_DH, (h + 1) * MEM_DH)
        s = _dot_nt(q[:, sl], mk_ref[:, sl]) * (MEM_DH ** -0.5)
        outs.append(_dot(_softmax_rows(s), mv_ref[:, sl]))
    o_ref[...] = h1 + _dot(jnp.concatenate(outs, axis=1), wo_ref[...])


def _s_attn_kernel(q_ref, k_ref, v_ref, o_ref):
    nq = MEM_HEADS * 4
    nk = N_MEM * MEM_HEADS
    own_head = (_iota((nq, nk), 1) & (MEM_HEADS - 1)) == (_iota((nq, nk), 0) >> 2)
    for r in range(ATT_ROWS):
        q = q_ref[4 * r:4 * r + 4, :]
        qh = jnp.concatenate([q[:, h * MEM_DH:(h + 1) * MEM_DH] for h in range(MEM_HEADS)], axis=0)
        kk = k_ref[r].reshape(nk, MEM_DH)
        vv = v_ref[r].reshape(nk, MEM_DH)
        s = jnp.where(own_head, _dot_nt(qh, kk) * (MEM_DH ** -0.5), -jnp.inf)
        o = _dot(_softmax_rows(s), vv)
        for h in range(MEM_HEADS):
            o_ref[4 * r:4 * r + 4, h * MEM_DH:(h + 1) * MEM_DH] = o[4 * h:4 * h + 4]


def _ffn_chunks(hn, wg_ref, wu_ref, cw_ref, cb_ref, wd_ref, prod_ref, shifts, sink):
    hb = hn.astype(BF16)
    for c in range(D_FF // FF_CHUNK):
        cs = slice(c * FF_CHUNK, (c + 1) * FF_CHUNK)
        g = jnp.dot(hb, wg_ref[:, cs], preferred_element_type=F32)
        s1, s2 = shifts(g, cs)
        conv = cb_ref[:, cs] + ((cw_ref[0:1, cs] * s2 + cw_ref[1:2, cs] * s1) + cw_ref[2:3, cs] * g)
        up = jnp.dot(hb, wu_ref[:, cs], preferred_element_type=F32)
        prod_ref[:, cs] = (_gelu(conv) * up).astype(BF16)
        sink(g, cs)
    return jnp.dot(prod_ref[...], wd_ref[...], preferred_element_type=F32)


def _p_ffn_kernel(h_ref, gffn_ref, wg_ref, wu_ref, cw_ref, cb_ref, wd_ref, gfin_ref,
                  y_ref, tail_ref, carry_ref, prod_ref):
    t = pl.program_id(1)

    @pl.when(t == 0)
    def _():
        carry_ref[...] = jnp.zeros_like(carry_ref)

    h2 = h_ref[...]
    m = h2.shape[0]
    row = _iota((m, FF_CHUNK), 0)

    def shifts(g, cs):
        prev = carry_ref[:, cs]
        p1 = prev[7:8, :]
        p2 = prev[6:7, :]
        s1 = jnp.where(row == 0, p1, pltpu.roll(g, 1, 0))
        s2 = jnp.where(row == 0, p2, jnp.where(row == 1, p1, pltpu.roll(g, 2, 0)))
        return s1, s2

    def sink(g, cs):
        carry_ref[:, cs] = g[m - 8:m, :]

    f = _ffn_chunks(_rms(h2, gffn_ref[...]), wg_ref, wu_ref, cw_ref, cb_ref, wd_ref, prod_ref, shifts, sink)
    y_ref[...] = _rms(h2 + f, gfin_ref[...])

    @pl.when(t == pl.num_programs(1) - 1)
    def _():
        tail_ref[...] = carry_ref[...]


def _s_ffn_kernel(h_ref, ao_ref, wo_ref, buf_ref, gffn_ref, wg_ref, wu_ref, cw_ref, cb_ref,
                  wd_ref, gfin_ref, y_ref, tail_ref, prod_ref):
    h2 = h_ref[...] + _dot(ao_ref[...], wo_ref[...])
    nrow2 = TT // 2
    tpos = _iota((TT, FF_CHUNK), 0) & 3
    tok = _iota((TT, nrow2), 0)
    ent = _iota((TT, nrow2), 1)
    same_row = (tok >> 2) == (ent >> 1)
    sel2 = (same_row & ((tok & 3) == (ent & 1))).astype(BF16)
    sel1 = (same_row & ((tok & 3) == 0) & ((ent & 1) == 1)).astype(BF16)
    sel_tail = (((_iota((nrow2, TT), 1) >> 2) == (_iota((nrow2, TT), 0) >> 1))
                & ((_iota((nrow2, TT), 1) & 3) == (_iota((nrow2, TT), 0) & 1) + 2)).astype(BF16)

    def shifts(g, cs):
        parts = _split3(buf_ref[:, :, cs].reshape(nrow2, FF_CHUNK))
        s1 = jnp.where(tpos == 0, _dot_exact_lhs(sel1, parts), pltpu.roll(g, 1, 0))
        s2 = jnp.where(tpos < 2, _dot_exact_lhs(sel2, parts), pltpu.roll(g, 2, 0))
        return s1, s2

    def sink(g, cs):
        tail_ref[:, :, cs] = _dot_exact_lhs(sel_tail, _split3(g)).reshape(TT // 4, 2, FF_CHUNK)

    f = _ffn_chunks(_rms(h2, gffn_ref[...]), wg_ref, wu_ref, cw_ref, cb_ref, wd_ref, prod_ref, shifts, sink)
    y_ref[...] = _rms(h2 + f, gfin_ref[...]).reshape(TT // 4, 4, D_MODEL)


def _const_spec(shape, single=True):
    nd = len(shape)
    kw = {"pipeline_mode": pl.Buffered(1)} if single else {}
    return pl.BlockSpec(shape, lambda *_: (0,) * nd, **kw)


def _params(sem):
    return pltpu.CompilerParams(dimension_semantics=sem, vmem_limit_bytes=VMEM_LIMIT)


def kernel(x_prompt, x_sample, mem_prompt, state_gla, state_conv, cache_mem_k, cache_mem_v, g_mix, w_in, w_alpha, b_alpha, g_gla_out, g_sgu, w_s, b_s, w_out, g_x, g_mem, wq_x, wk_x, wv_x, wo_x, g_ffn, w_gate, w_up, conv_w, conv_b, w_down, g_final):
    B, T, _ = x_prompt.shape
    RB, TS, _ = x_sample.shape
    assert T % TMIX == 0 and TMIX % TT == 0 and T % TBIG == 0
    assert TS == 4 and (RB * TS) % TT == 0 and RB % ATT_ROWS == 0
    ntok_s = RB * TS
    rows_per_step = TT // TS

    w_in_b = w_in[0].astype(BF16)
    cut = 2 * QK_COLS + 2 * GLA_WIDTH
    w_qkvr = w_in_b[:, :cut]
    w_usv = w_in_b[:, cut + GLA_RANK:]
    w_a = jnp.pad(w_in_b[:, cut:cut + GLA_RANK], ((0, 0), (0, RANK_PAD - GLA_RANK)))
    w_alpha_p = jnp.concatenate([w_alpha[0], jnp.zeros((RANK_PAD - GLA_RANK, QK_COLS), F32)], axis=0).astype(BF16)
    b_alpha_r = b_alpha[0].reshape(1, QK_COLS)
    gmix = g_mix[0].reshape(1, D_MODEL)
    ggla = g_gla_out[0].reshape(1, GLA_WIDTH)
    gsgu = g_sgu[0].reshape(1, SGU_WIDTH)
    gx = g_x[0].reshape(1, D_MODEL)
    gmem = g_mem[0].reshape(1, D_MODEL)
    gffn = g_ffn[0].reshape(1, D_MODEL)
    gfin = g_final.reshape(1, D_MODEL)
    w_out_b = w_out[0].astype(BF16)
    wq_b, wk_b, wv_b, wo_b = (w[0].astype(BF16) for w in (wq_x, wk_x, wv_x, wo_x))
    wg_b, wu_b, wd_b = (w[0].astype(BF16) for w in (w_gate, w_up, w_down))
    cw = conv_w[0]
    cb = conv_b[0].reshape(1, D_FF)
    ws_p = jnp.tile(w_s[0], (1, TT // SGU_CHUNK, TT // SGU_CHUNK)).astype(BF16)
    bs_p = jnp.tile(jnp.repeat(b_s[0].T, SGU_DH, axis=1), (TT // SGU_CHUNK, 1))
    w4 = w_s[0][:, :TS, :TS]
    ws_s = jnp.stack([
        jnp.tile(jnp.repeat(jnp.pad(jnp.diagonal(w4, offset=-kk, axis1=1, axis2=2), ((0, 0), (kk, 0))).T,
                            SGU_DH, axis=1), (2, 1))
        for kk in range(TS)])
    bs_s = jnp.tile(jnp.repeat(b_s[0][:, :TS].T, SGU_DH, axis=1), (2, 1))

    mixer_w_specs = [
        _const_spec((1, D_MODEL)), _const_spec((D_MODEL, cut)), _const_spec((D_MODEL, 2 * SGU_WIDTH)),
        _const_spec((D_MODEL, RANK_PAD)), _const_spec((RANK_PAD, QK_COLS)),
        _const_spec((1, QK_COLS)), _const_spec((1, GLA_WIDTH)), _const_spec((1, SGU_WIDTH)),
    ]
    mixer_w = (gmix, w_qkvr, w_usv, w_a, w_alpha_p, b_alpha_r, ggla, gsgu)

    kv4_spec = pl.BlockSpec((None, N_MEM, MEM_HEADS, MEM_DH), lambda b: (b, 0, 0, 0))
    mk_p, mv_p, mkb_p, mvb_p = pl.pallas_call(
        _memkv_kernel,
        grid=(B,),
        in_specs=[pl.BlockSpec((None, N_MEM, D_MODEL), lambda b: (b, 0, 0)),
                  _const_spec((1, D_MODEL)), _const_spec((D_MODEL, D_MODEL)), _const_spec((D_MODEL, D_MODEL))],
        out_specs=[kv4_spec, kv4_spec] + [pl.BlockSpec((None, N_MEM, D_MODEL), lambda b: (b, 0, 0))] * 2,
        out_shape=[jax.ShapeDtypeStruct((B, N_MEM, MEM_HEADS, MEM_DH), F32)] * 2
        + [jax.ShapeDtypeStruct((B, N_MEM, D_MODEL), BF16)] * 2,
        compiler_params=_params(("arbitrary",)),
        name="memkv",
    )(mem_prompt, gmem, wk_b, wv_b)

    tile_spec = pl.BlockSpec((None, TMIX, D_MODEL), lambda b, t: (b, t, 0))
    h1_p, s_p = pl.pallas_call(
        _p_mixer_kernel,
        grid=(B, T // TMIX),
        in_specs=[tile_spec] + mixer_w_specs + [
            _const_spec((SGU_HEADS, TT, TT)), _const_spec((TT, SGU_WIDTH)), _const_spec((D_MODEL, D_MODEL))],
        out_specs=[tile_spec,
                   pl.BlockSpec((None, GLA_HEADS, GLA_DK, GLA_DV), lambda b, t: (b, 0, 0, 0))],
        out_shape=[jax.ShapeDtypeStruct((B, T, D_MODEL), F32),
                   jax.ShapeDtypeStruct((B, GLA_HEADS, GLA_DK, GLA_DV), F32)],
        scratch_shapes=[pltpu.VMEM((GLA_WIDTH, QK_COLS), F32)],
        compiler_params=_params(("arbitrary", "arbitrary")),
        name="p_mixer",
    )(x_prompt, *mixer_w, ws_p, bs_p, w_out_b)

    kv_spec = pl.BlockSpec((None, N_MEM, D_MODEL), lambda b, t: (b, 0, 0))
    big_spec = pl.BlockSpec((None, TBIG, D_MODEL), lambda b, t: (b, t, 0))
    h2_p = pl.pallas_call(
        _p_attn_kernel,
        grid=(B, T // TBIG),
        in_specs=[big_spec, _const_spec((1, D_MODEL)), _const_spec((D_MODEL, D_MODEL)),
                  _const_spec((D_MODEL, D_MODEL)), kv_spec, kv_spec],
        out_specs=big_spec,
        out_shape=jax.ShapeDtypeStruct((B, T, D_MODEL), F32),
        compiler_params=_params(("arbitrary", "arbitrary")),
        name="p_attn",
    )(h1_p, gx, wq_b, wo_b, mkb_p, mvb_p)

    ffn_w_specs = [_const_spec((1, D_MODEL)), _const_spec((D_MODEL, D_FF)), _const_spec((D_MODEL, D_FF)),
                   _const_spec((CONV_W, D_FF)), _const_spec((1, D_FF)), _const_spec((D_FF, D_MODEL)),
                   _const_spec((1, D_MODEL))]
    y_p, tail_p = pl.pallas_call(
        _p_ffn_kernel,
        grid=(B, T // TBIG),
        in_specs=[big_spec] + ffn_w_specs,
        out_specs=[big_spec, pl.BlockSpec((None, 8, D_FF), lambda b, t: (b, 0, 0))],
        out_shape=[jax.ShapeDtypeStruct((B, T, D_MODEL), F32), jax.ShapeDtypeStruct((B, 8, D_FF), F32)],
        scratch_shapes=[pltpu.VMEM((8, D_FF), F32), pltpu.VMEM((TBIG, D_FF), BF16)],
        compiler_params=_params(("arbitrary", "arbitrary")),
        name="p_ffn",
    )(h2_p, gffn, wg_b, wu_b, cw, cb, wd_b, gfin)

    s_in = state_gla[0]
    row3_spec = pl.BlockSpec((rows_per_step, TS, D_MODEL), lambda i: (i, 0, 0))
    tok_spec = pl.BlockSpec((TT, D_MODEL), lambda i: (i, 0))
    st_spec = pl.BlockSpec((rows_per_step, GLA_HEADS, GLA_DK, GLA_DV), lambda i: (i, 0, 0, 0))
    h1_s, q_s, sv_s, s_s = pl.pallas_call(
        _s_mixer_kernel,
        grid=(ntok_s // TT,),
        in_specs=[row3_spec, st_spec] + mixer_w_specs + [
            _const_spec((TS, 8, SGU_WIDTH)), _const_spec((8, SGU_WIDTH)), _const_spec((D_MODEL, D_MODEL)),
            _const_spec((1, D_MODEL)), _const_spec((D_MODEL, D_MODEL))],
        out_specs=[tok_spec, tok_spec,
                   pl.BlockSpec((rows_per_step, TS, SGU_HEADS, SGU_DH), lambda i: (i, 0, 0, 0)), st_spec],
        out_shape=[jax.ShapeDtypeStruct((ntok_s, D_MODEL), F32), jax.ShapeDtypeStruct((ntok_s, D_MODEL), F32),
                   jax.ShapeDtypeStruct((RB, TS, SGU_HEADS, SGU_DH), F32),
                   jax.ShapeDtypeStruct((RB, GLA_HEADS, GLA_DK, GLA_DV), F32)],
        compiler_params=_params(("arbitrary",)),
        name="s_mixer",
    )(x_sample, s_in, *mixer_w, ws_s, bs_s, w_out_b, gx, wq_b)

    ck = cache_mem_k[0]
    cv = cache_mem_v[0]
    qa_spec = pl.BlockSpec((ATT_ROWS * TS, D_MODEL), lambda i: (i, 0))
    ckv_spec = pl.BlockSpec((ATT_ROWS, N_MEM, MEM_HEADS, MEM_DH), lambda i: (i, 0, 0, 0))
    ao_s = pl.pallas_call(
        _s_attn_kernel,
        grid=(RB // ATT_ROWS,),
        in_specs=[qa_spec, ckv_spec, ckv_spec],
        out_specs=qa_spec,
        out_shape=jax.ShapeDtypeStruct((ntok_s, D_MODEL), F32),
        compiler_params=_params(("arbitrary",)),
        name="s_attn",
    )(q_s, ck, cv)

    hist_spec = pl.BlockSpec((rows_per_step, CONV_W - 1, D_FF), lambda i: (i, 0, 0))
    y_s, tail_s = pl.pallas_call(
        _s_ffn_kernel,
        grid=(ntok_s // TT,),
        in_specs=[tok_spec, tok_spec, _const_spec((D_MODEL, D_MODEL)), hist_spec] + ffn_w_specs,
        out_specs=[row3_spec, hist_spec],
        out_shape=[jax.ShapeDtypeStruct((RB, TS, D_MODEL), F32),
                   jax.ShapeDtypeStruct((RB, CONV_W - 1, D_FF), F32)],
        scratch_shapes=[pltpu.VMEM((TT, D_FF), BF16)],
        compiler_params=_params(("arbitrary",)),
        name="s_ffn",
    )(h1_s, ao_s, wo_b, state_conv[0], gffn, wg_b, wu_b, cw, cb, wd_b, gfin)

    return (y_p, y_s, s_p[None], tail_p[:, 6:8][None], mk_p[None], mv_p[None], s_s[None], tail_s[None], sv_s[None])
```

```python
import functools

import jax
import jax.numpy as jnp
from jax import lax
from jax.experimental import pallas as pl
from jax.experimental.pallas import tpu as pltpu

F32 = jnp.float32
BF16 = jnp.bfloat16

D_MODEL = 1024
GLA_HEADS = 4
GLA_DK = 64
GLA_DV = 128
QK_COLS = GLA_HEADS * GLA_DK
GLA_WIDTH = GLA_HEADS * GLA_DV
GLA_RANK = 16
GLA_TAU = 16.0
SGU_HEADS = 4
SGU_DH = 128
SGU_WIDTH = SGU_HEADS * SGU_DH
SGU_CHUNK = 128
N_MEM = 256
MEM_HEADS = 4
MEM_DH = 256
D_FF = 2816
CONV_W = 3
EPS = 1e-6

LANES = 128
RANK_PAD = LANES
TT = 256
TBIG = 1024
TMIX = 1024
GLA_BLOCK = 64
SROWS = 32
STOK = SROWS * 4
FF_CHUNK = 256
TATT = 512
ATT_ROWS = 4
VMEM_LIMIT = 56 * 1024 * 1024


def _dot(a, b):
    return jnp.dot(a.astype(BF16), b.astype(BF16), preferred_element_type=F32)


def _dot_nt(a, b):
    return lax.dot_general(a.astype(BF16), b.astype(BF16), (((1,), (1,)), ((), ())),
                           preferred_element_type=F32)


def _dot_tn(a, b):
    return lax.dot_general(a.astype(BF16), b.astype(BF16), (((0,), (0,)), ((), ())),
                           preferred_element_type=F32)


def _split3(x):
    hi = x.astype(BF16)
    r1 = x - hi.astype(F32)
    mid = r1.astype(BF16)
    lo = (r1 - mid.astype(F32)).astype(BF16)
    return hi, mid, lo


def _dot_exact_lhs(sel, parts):
    hi, mid, lo = parts
    return (jnp.dot(sel, hi, preferred_element_type=F32)
            + jnp.dot(sel, mid, preferred_element_type=F32)
            + jnp.dot(sel, lo, preferred_element_type=F32))


def _rms(x, g):
    ms = jnp.mean(x * x, axis=-1, keepdims=True)
    return x * lax.rsqrt(ms + EPS) * g


def _rms_heads(x, g, nh, dh):
    outs = []
    for h in range(nh):
        xh = x[:, h * dh:(h + 1) * dh]
        ms = jnp.mean(xh * xh, axis=-1, keepdims=True)
        outs.append(xh * lax.rsqrt(ms + EPS))
    return jnp.concatenate(outs, axis=-1) * g


def _gelu(x):
    c = 0.7978845608028654
    return x * (0.5 * (1.0 + jnp.tanh(c * (x + 0.044715 * (x * x * x)))))


def _silu(x):
    return x * (1.0 / (1.0 + jnp.exp(-x)))


def _log_sigmoid(x):
    return jnp.minimum(x, 0.0) - jnp.log1p(jnp.exp(-jnp.abs(x)))


def _iota(shape, dim):
    return lax.broadcasted_iota(jnp.int32, shape, dim)


def _in_proj(x, gmix, w_qkvr, w_usv, w_a, w_alpha, b_alpha):
    xn = _rms(x, gmix).astype(BF16)
    p = jnp.dot(xn, w_qkvr, preferred_element_type=F32)
    q = p[:, 0:256] * (GLA_DK ** -0.5)
    k = p[:, 256:512]
    v = p[:, 512:1024]
    r = p[:, 1024:1536]
    a = jnp.dot(xn, w_a, preferred_element_type=F32)
    xg = _dot(a, w_alpha) + b_alpha
    logg = _log_sigmoid(xg) * (1.0 / GLA_TAU)
    p2 = jnp.dot(xn, w_usv, preferred_element_type=F32)
    u = p2[:, 0:512]
    sv = p2[:, 512:1024]
    return q, k, v, r, u, sv, logg


def _mix_out(x, o, r, u, z, ggla, w_out):
    og = _rms_heads(o, ggla, GLA_HEADS, GLA_DV) * _silu(r)
    s_out = u * z
    y = _dot(og, w_out[0:GLA_WIDTH, :]) + _dot(s_out, w_out[GLA_WIDTH:, :])
    return x + y


def _gla_prompt_tile(q, k, v, logg, st_ref):
    n = TT
    ri = _iota((n, n), 0)
    ci = _iota((n, n), 1)
    low = (ci <= ri).astype(BF16)
    bt = _dot_exact_lhs(low, _split3(logg))
    headm = (ri >> 6) == (ci >> 6)
    outs = []
    for blk in range(n // GLA_BLOCK):
        r0 = blk * GLA_BLOCK
        r1 = r0 + GLA_BLOCK
        nn = -(-r1 // LANES) * LANES
        if blk == 0:
            bq = bt[0:GLA_BLOCK]
            ek = jnp.exp(-bt[0:nn])
        else:
            ref = bt[r0 - 1:r0, :]
            bq = bt[r0:r1] - ref
            ek = jnp.exp(ref - bt[0:nn])
        qb = q[r0:r1] * jnp.exp(bq)
        qs = jnp.where(headm, jnp.concatenate([qb] * GLA_HEADS, axis=0), 0.0)
        kb = k[0:nn] * ek
        sc = _dot_nt(qs, kb)
        cm = _iota((n, nn), 1) <= (_iota((n, nn), 0) & (GLA_BLOCK - 1)) + r0
        sc = jnp.where(cm, sc, 0.0)
        ov = _dot(sc, v[0:nn])
        outs.append(jnp.concatenate(
            [ov[h * GLA_BLOCK:(h + 1) * GLA_BLOCK, h * GLA_DV:(h + 1) * GLA_DV] for h in range(GLA_HEADS)],
            axis=1))
    o_intra = jnp.concatenate(outs, axis=0)
    st = st_ref[...]
    o_state = _dot_nt(q * jnp.exp(bt), st)
    bl = bt[n - 1:n, :]
    kh = k * jnp.exp(bl - bt)
    upd = _dot_tn(v, kh)
    bdm = (_iota((GLA_WIDTH, QK_COLS), 0) >> 7) == (_iota((GLA_WIDTH, QK_COLS), 1) >> 6)
    st_ref[...] = jnp.exp(bl) * st + jnp.where(bdm, upd, 0.0)
    return o_intra + o_state


def _p_mixer_kernel(x_ref, gmix_ref, wqkvr_ref, wusv_ref, wa_ref, walpha_ref, balpha_ref, ggla_ref, gsgu_ref,
                    ws_ref, bs_ref, wout_ref, h_ref, s_ref, st_ref):
    t = pl.program_id(1)

    @pl.when(t == 0)
    def _():
        st_ref[...] = jnp.zeros_like(st_ref)

    x = x_ref[...]
    q, k, v, r, u, sv, logg = _in_proj(x, gmix_ref[...], wqkvr_ref[...], wusv_ref[...], wa_ref[...],
                                       walpha_ref[...], balpha_ref[...])
    u = _gelu(u)
    svn = _rms_heads(_gelu(sv), gsgu_ref[...], SGU_HEADS, SGU_DH)
    ri = _iota((TT, TT), 0)
    ci = _iota((TT, TT), 1)
    wm = ((ri >> 7) == (ci >> 7)) & (ci <= ri)
    ws = [jnp.where(wm, ws_ref[h].astype(F32), 0.0).astype(BF16) for h in range(SGU_HEADS)]
    os_, zs = [], []
    for sub in range(x.shape[0] // TT):
        ts = slice(sub * TT, (sub + 1) * TT)
        os_.append(_gla_prompt_tile(q[ts], k[ts], v[ts], logg[ts], st_ref))
        zh = [_dot(ws[h], svn[ts, h * SGU_DH:(h + 1) * SGU_DH]) for h in range(SGU_HEADS)]
        zs.append(jnp.concatenate(zh, axis=1) + bs_ref[...])
    o = jnp.concatenate(os_, axis=0)
    z = jnp.concatenate(zs, axis=0)
    h_ref[...] = _mix_out(x, o, r, u, z, ggla_ref[...], wout_ref)

    @pl.when(t == pl.num_programs(1) - 1)
    def _():
        s_full = st_ref[...].T
        for h in range(GLA_HEADS):
            s_ref[h] = s_full[h * GLA_DK:(h + 1) * GLA_DK, h * GLA_DV:(h + 1) * GLA_DV]


def _gla_sample_block(q, k, v, logg, sin_ref, sout_ref, row0):
    n = STOK
    ri = _iota((n, n), 0)
    ci = _iota((n, n), 1)
    same = (ri >> 2) == (ci >> 2)
    causal = same & (ci <= ri)
    parts = _split3(logg)
    b = _dot_exact_lhs(causal.astype(BF16), parts)
    bl = _dot_exact_lhs(same.astype(BF16), parts)
    qt = q * jnp.exp(b)
    kt = k * jnp.exp(-b)
    kh = k * jnp.exp(bl - b)
    headm = (_iota((GLA_HEADS * n, QK_COLS), 0) >> 7) == (_iota((GLA_HEADS * n, QK_COLS), 1) >> 6)
    qs = jnp.where(headm, jnp.concatenate([qt] * GLA_HEADS, axis=0), 0.0)
    sc = _dot_nt(qs, kt)
    tok = _iota((GLA_HEADS * n, n), 0) & (n - 1)
    col = _iota((GLA_HEADS * n, n), 1)
    sc = jnp.where(((tok >> 2) == (col >> 2)) & (col <= tok), sc, 0.0)
    ov = _dot(sc, v)
    o_intra = jnp.concatenate(
        [ov[h * n:(h + 1) * n, h * GLA_DV:(h + 1) * GLA_DV] for h in range(GLA_HEADS)], axis=1)

    nexp = SROWS * GLA_DK
    expand = ((_iota((GLA_DK, nexp), 1) & (GLA_DK - 1)) == _iota((GLA_DK, nexp), 0)).astype(BF16)
    expand_t = ((_iota((nexp, GLA_DK), 0) & (GLA_DK - 1)) == _iota((nexp, GLA_DK), 1)).astype(BF16)
    bd = (_iota((n, nexp), 0) >> 2) == (_iota((n, nexp), 1) >> 6)
    bd_t = (_iota((nexp, n), 0) >> 6) == (_iota((nexp, n), 1) >> 2)
    dec_t = jnp.exp(bl).T
    kh_t = kh.T
    outs = []
    for h in range(GLA_HEADS):
        dsl = slice(h * GLA_DK, (h + 1) * GLA_DK)
        s_in = sin_ref[row0:row0 + SROWS, h].reshape(nexp, GLA_DV)
        qe = jnp.where(bd, _dot(qt[:, dsl], expand), 0.0)
        outs.append(_dot(qe, s_in))
        ke_t = jnp.where(bd_t, _dot(expand_t, kh_t[dsl, :]), 0.0)
        upd = _dot(ke_t, v[:, h * GLA_DV:(h + 1) * GLA_DV])
        for rr in range(SROWS):
            dcol = jnp.broadcast_to(dec_t[dsl, 4 * rr:4 * rr + 1], (GLA_DK, GLA_DV))
            sout_ref[row0 + rr, h] = dcol * sin_ref[row0 + rr, h] + upd[rr * GLA_DK:(rr + 1) * GLA_DK]
    return o_intra + jnp.concatenate(outs, axis=1)


def _s_mixer_kernel(x_ref, sin_ref, gmix_ref, wqkvr_ref, wusv_ref, wa_ref, walpha_ref, balpha_ref, ggla_ref,
                    gsgu_ref, wsp_ref, bsp_ref, wout_ref, gx_ref, wq_ref,
                    h_ref, q_ref, sv_ref, sout_ref):
    x = x_ref[...].reshape(TT, D_MODEL)
    q, k, v, r, u, sv, logg = _in_proj(x, gmix_ref[...], wqkvr_ref[...], wusv_ref[...], wa_ref[...],
                                       walpha_ref[...], balpha_ref[...])
    u = _gelu(u)
    svn = _rms_heads(_gelu(sv), gsgu_ref[...], SGU_HEADS, SGU_DH)
    sv_ref[...] = svn.reshape(TT // 4, 4, SGU_HEADS, SGU_DH)
    z = jnp.tile(bsp_ref[...], (TT // 8, 1))
    for kk in range(4):
        shifted = svn if kk == 0 else pltpu.roll(svn, kk, 0)
        z = z + jnp.tile(wsp_ref[kk], (TT // 8, 1)) * shifted
    os_ = []
    for sb in range(TT // STOK):
        ts = slice(sb * STOK, (sb + 1) * STOK)
        os_.append(_gla_sample_block(q[ts], k[ts], v[ts], logg[ts], sin_ref, sout_ref, sb * SROWS))
    o = jnp.concatenate(os_, axis=0)
    h1 = _mix_out(x, o, r, u, z, ggla_ref[...], wout_ref)
    h_ref[...] = h1
    q_ref[...] = _dot(_rms(h1, gx_ref[...]), wq_ref[...])


def _softmax_rows(s):
    m = jnp.max(s, axis=-1, keepdims=True)
    e = jnp.exp(s - m)
    return e / jnp.sum(e, axis=-1, keepdims=True)


def _memkv_kernel(mem_ref, gmem_ref, wk_ref, wv_ref, mk_ref, mv_ref, mkb_ref, mvb_ref):
    mn = _rms(mem_ref[...], gmem_ref[...])
    mk = _dot(mn, wk_ref[...])
    mv = _dot(mn, wv_ref[...])
    mk_ref[...] = mk.reshape(N_MEM, MEM_HEADS, MEM_DH)
    mv_ref[...] = mv.reshape(N_MEM, MEM_HEADS, MEM_DH)
    mkb_ref[...] = mk.astype(BF16)
    mvb_ref[...] = mv.astype(BF16)


def _attn_kernel(h_ref, gx_ref, wq_ref, wo_ref, mk_ref, mv_ref, qs_ref, ck_ref, cv_ref, o_ref, aos_ref):
    nq = MEM_HEADS * 4
    nk = N_MEM * MEM_HEADS
    own_head = (_iota((nq, nk), 1) & (MEM_HEADS - 1)) == (_iota((nq, nk), 0) >> 2)
    h1 = h_ref[...]
    q = _dot(_rms(h1, gx_ref[...]), wq_ref[...])

    scores = []
    for r in range(ATT_ROWS):
        qr = qs_ref[4 * r:4 * r + 4, :]
        qh = jnp.concatenate([qr[:, h * MEM_DH:(h + 1) * MEM_DH] for h in range(MEM_HEADS)], axis=0)
        scores.append(_dot_nt(qh, ck_ref[r].reshape(nk, MEM_DH)))

    sl = [slice(h * MEM_DH, (h + 1) * MEM_DH) for h in range(MEM_HEADS)]
    ps = [_dot_nt(q[:, sl[h]], mk_ref[:, sl[h]]) for h in range(MEM_HEADS)]

    for r in range(ATT_ROWS):
        p = _softmax_rows(jnp.where(own_head, scores[r] * (MEM_DH ** -0.5), -jnp.inf))
        o = _dot(p, cv_ref[r].reshape(nk, MEM_DH))
        for h in range(MEM_HEADS):
            aos_ref[4 * r:4 * r + 4, h * MEM_DH:(h + 1) * MEM_DH] = o[4 * h:4 * h + 4]

    outs = [_dot(_softmax_rows(ps[h] * (MEM_DH ** -0.5)), mv_ref[:, sl[h]]) for h in range(MEM_HEADS)]
    o_ref[...] = h1 + _dot(jnp.concatenate(outs, axis=1), wo_ref[...])


def _ffn_chunks(hn, wg_ref, wu_ref, cw_ref, cb_ref, wd_ref, prod_ref, shifts, sink):
    hb = hn.astype(BF16)
    for c in range(D_FF // FF_CHUNK):
        cs = slice(c * FF_CHUNK, (c + 1) * FF_CHUNK)
        g = jnp.dot(hb, wg_ref[:, cs], preferred_element_type=F32)
        s1, s2 = shifts(g, cs)
        conv = cb_ref[:, cs] + ((cw_ref[0:1, cs] * s2 + cw_ref[1:2, cs] * s1) + cw_ref[2:3, cs] * g)
        up = jnp.dot(hb, wu_ref[:, cs], preferred_element_type=F32)
        prod_ref[:, cs] = (_gelu(conv) * up).astype(BF16)
        sink(g, cs)
    return jnp.dot(prod_ref[...], wd_ref[...], preferred_element_type=F32)


def _p_ffn_kernel(h_ref, gffn_ref, wg_ref, wu_ref, cw_ref, cb_ref, wd_ref, gfin_ref,
                  y_ref, tail_ref, carry_ref, prod_ref):
    t = pl.program_id(1)

    @pl.when(t == 0)
    def _():
        carry_ref[...] = jnp.zeros_like(carry_ref)

    h2 = h_ref[...]
    m = h2.shape[0]
    row = _iota((m, FF_CHUNK), 0)

    def shifts(g, cs):
        prev = carry_ref[:, cs]
        p1 = prev[7:8, :]
        p2 = prev[6:7, :]
        s1 = jnp.where(row == 0, p1, pltpu.roll(g, 1, 0))
        s2 = jnp.where(row == 0, p2, jnp.where(row == 1, p1, pltpu.roll(g, 2, 0)))
        return s1, s2

    def sink(g, cs):
        carry_ref[:, cs] = g[m - 8:m, :]

    f = _ffn_chunks(_rms(h2, gffn_ref[...]), wg_ref, wu_ref, cw_ref, cb_ref, wd_ref, prod_ref, shifts, sink)
    y_ref[...] = _rms(h2 + f, gfin_ref[...])

    @pl.when(t == pl.num_programs(1) - 1)
    def _():
        tail_ref[...] = carry_ref[...]


def _s_ffn_kernel(h_ref, ao_ref, wo_ref, buf_ref, gffn_ref, wg_ref, wu_ref, cw_ref, cb_ref,
                  wd_ref, gfin_ref, y_ref, tail_ref, prod_ref):
    h2 = h_ref[...] + _dot(ao_ref[...], wo_ref[...])
    nrow2 = TT // 2
    tpos = _iota((TT, FF_CHUNK), 0) & 3
    tok = _iota((TT, nrow2), 0)
    ent = _iota((TT, nrow2), 1)
    same_row = (tok >> 2) == (ent >> 1)
    sel2 = (same_row & ((tok & 3) == (ent & 1))).astype(BF16)
    sel1 = (same_row & ((tok & 3) == 0) & ((ent & 1) == 1)).astype(BF16)
    sel_tail = (((_iota((nrow2, TT), 1) >> 2) == (_iota((nrow2, TT), 0) >> 1))
                & ((_iota((nrow2, TT), 1) & 3) == (_iota((nrow2, TT), 0) & 1) + 2)).astype(BF16)

    def shifts(g, cs):
        parts = _split3(buf_ref[:, :, cs].reshape(nrow2, FF_CHUNK))
        s1 = jnp.where(tpos == 0, _dot_exact_lhs(sel1, parts), pltpu.roll(g, 1, 0))
        s2 = jnp.where(tpos < 2, _dot_exact_lhs(sel2, parts), pltpu.roll(g, 2, 0))
        return s1, s2

    def sink(g, cs):
        tail_ref[:, :, cs] = _dot_exact_lhs(sel_tail, _split3(g)).reshape(TT // 4, 2, FF_CHUNK)

    f = _ffn_chunks(_rms(h2, gffn_ref[...]), wg_ref, wu_ref, cw_ref, cb_ref, wd_ref, prod_ref, shifts, sink)
    y_ref[...] = _rms(h2 + f, gfin_ref[...]).reshape(TT // 4, 4, D_MODEL)


def _const_spec(shape, single=True):
    nd = len(shape)
    kw = {"pipeline_mode": pl.Buffered(1)} if single else {}
    return pl.BlockSpec(shape, lambda *_: (0,) * nd, **kw)


def _params(sem):
    return pltpu.CompilerParams(dimension_semantics=sem, vmem_limit_bytes=VMEM_LIMIT)


def kernel(x_prompt, x_sample, mem_prompt, state_gla, state_conv, cache_mem_k, cache_mem_v, g_mix, w_in, w_alpha, b_alpha, g_gla_out, g_sgu, w_s, b_s, w_out, g_x, g_mem, wq_x, wk_x, wv_x, wo_x, g_ffn, w_gate, w_up, conv_w, conv_b, w_down, g_final):
    B, T, _ = x_prompt.shape
    RB, TS, _ = x_sample.shape
    assert T % TMIX == 0 and TMIX % TT == 0 and T % TBIG == 0
    assert TS == 4 and (RB * TS) % TT == 0 and RB % ATT_ROWS == 0
    ntok_s = RB * TS
    rows_per_step = TT // TS

    w_in_b = w_in[0].astype(BF16)
    cut = 2 * QK_COLS + 2 * GLA_WIDTH
    w_qkvr = w_in_b[:, :cut]
    w_usv = w_in_b[:, cut + GLA_RANK:]
    w_a = jnp.pad(w_in_b[:, cut:cut + GLA_RANK], ((0, 0), (0, RANK_PAD - GLA_RANK)))
    w_alpha_p = jnp.concatenate([w_alpha[0], jnp.zeros((RANK_PAD - GLA_RANK, QK_COLS), F32)], axis=0).astype(BF16)
    b_alpha_r = b_alpha[0].reshape(1, QK_COLS)
    gmix = g_mix[0].reshape(1, D_MODEL)
    ggla = g_gla_out[0].reshape(1, GLA_WIDTH)
    gsgu = g_sgu[0].reshape(1, SGU_WIDTH)
    gx = g_x[0].reshape(1, D_MODEL)
    gmem = g_mem[0].reshape(1, D_MODEL)
    gffn = g_ffn[0].reshape(1, D_MODEL)
    gfin = g_final.reshape(1, D_MODEL)
    w_out_b = w_out[0].astype(BF16)
    wq_b, wk_b, wv_b, wo_b = (w[0].astype(BF16) for w in (wq_x, wk_x, wv_x, wo_x))
    wg_b, wu_b, wd_b = (w[0].astype(BF16) for w in (w_gate, w_up, w_down))
    cw = conv_w[0]
    cb = conv_b[0].reshape(1, D_FF)
    ws_p = jnp.tile(w_s[0], (1, TT // SGU_CHUNK, TT // SGU_CHUNK)).astype(BF16)
    bs_p = jnp.tile(jnp.repeat(b_s[0].T, SGU_DH, axis=1), (TT // SGU_CHUNK, 1))
    w4 = w_s[0][:, :TS, :TS]
    ws_s = jnp.stack([
        jnp.tile(jnp.repeat(jnp.pad(jnp.diagonal(w4, offset=-kk, axis1=1, axis2=2), ((0, 0), (kk, 0))).T,
                            SGU_DH, axis=1), (2, 1))
        for kk in range(TS)])
    bs_s = jnp.tile(jnp.repeat(b_s[0][:, :TS].T, SGU_DH, axis=1), (2, 1))

    mixer_w_specs = [
        _const_spec((1, D_MODEL)), _const_spec((D_MODEL, cut)), _const_spec((D_MODEL, 2 * SGU_WIDTH)),
        _const_spec((D_MODEL, RANK_PAD)), _const_spec((RANK_PAD, QK_COLS)),
        _const_spec((1, QK_COLS)), _const_spec((1, GLA_WIDTH)), _const_spec((1, SGU_WIDTH)),
    ]
    mixer_w = (gmix, w_qkvr, w_usv, w_a, w_alpha_p, b_alpha_r, ggla, gsgu)

    kv4_spec = pl.BlockSpec((None, N_MEM, MEM_HEADS, MEM_DH), lambda b: (b, 0, 0, 0))
    mk_p, mv_p, mkb_p, mvb_p = pl.pallas_call(
        _memkv_kernel,
        grid=(B,),
        in_specs=[pl.BlockSpec((None, N_MEM, D_MODEL), lambda b: (b, 0, 0)),
                  _const_spec((1, D_MODEL)), _const_spec((D_MODEL, D_MODEL)), _const_spec((D_MODEL, D_MODEL))],
        out_specs=[kv4_spec, kv4_spec] + [pl.BlockSpec((None, N_MEM, D_MODEL), lambda b: (b, 0, 0))] * 2,
        out_shape=[jax.ShapeDtypeStruct((B, N_MEM, MEM_HEADS, MEM_DH), F32)] * 2
        + [jax.ShapeDtypeStruct((B, N_MEM, D_MODEL), BF16)] * 2,
        compiler_params=_params(("arbitrary",)),
        name="memkv",
    )(mem_prompt, gmem, wk_b, wv_b)

    tile_spec = pl.BlockSpec((None, TMIX, D_MODEL), lambda b, t: (b, t, 0))
    h1_p, s_p = pl.pallas_call(
        _p_mixer_kernel,
        grid=(B, T // TMIX),
        in_specs=[tile_spec] + mixer_w_specs + [
            _const_spec((SGU_HEADS, TT, TT)), _const_spec((TT, SGU_WIDTH)), _const_spec((D_MODEL, D_MODEL))],
        out_specs=[tile_spec,
                   pl.BlockSpec((None, GLA_HEADS, GLA_DK, GLA_DV), lambda b, t: (b, 0, 0, 0))],
        out_shape=[jax.ShapeDtypeStruct((B, T, D_MODEL), F32),
                   jax.ShapeDtypeStruct((B, GLA_HEADS, GLA_DK, GLA_DV), F32)],
        scratch_shapes=[pltpu.VMEM((GLA_WIDTH, QK_COLS), F32)],
        compiler_params=_params(("arbitrary", "arbitrary")),
        name="p_mixer",
    )(x_prompt, *mixer_w, ws_p, bs_p, w_out_b)

    s_in = state_gla[0]
    row3_spec = pl.BlockSpec((rows_per_step, TS, D_MODEL), lambda i: (i, 0, 0))
    tok_spec = pl.BlockSpec((TT, D_MODEL), lambda i: (i, 0))
    st_spec = pl.BlockSpec((rows_per_step, GLA_HEADS, GLA_DK, GLA_DV), lambda i: (i, 0, 0, 0))
    h1_s, q_s, sv_s, s_s = pl.pallas_call(
        _s_mixer_kernel,
        grid=(ntok_s // TT,),
        in_specs=[row3_spec, st_spec] + mixer_w_specs + [
            _const_spec((TS, 8, SGU_WIDTH)), _const_spec((8, SGU_WIDTH)), _const_spec((D_MODEL, D_MODEL)),
            _const_spec((1, D_MODEL)), _const_spec((D_MODEL, D_MODEL))],
        out_specs=[tok_spec, tok_spec,
                   pl.BlockSpec((rows_per_step, TS, SGU_HEADS, SGU_DH), lambda i: (i, 0, 0, 0)), st_spec],
        out_shape=[jax.ShapeDtypeStruct((ntok_s, D_MODEL), F32), jax.ShapeDtypeStruct((ntok_s, D_MODEL), F32),
                   jax.ShapeDtypeStruct((RB, TS, SGU_HEADS, SGU_DH), F32),
                   jax.ShapeDtypeStruct((RB, GLA_HEADS, GLA_DK, GLA_DV), F32)],
        compiler_params=_params(("arbitrary",)),
        name="s_mixer",
    )(x_sample, s_in, *mixer_w, ws_s, bs_s, w_out_b, gx, wq_b)

    nta = T // TATT
    assert B * nta * ATT_ROWS == RB
    att_spec = pl.BlockSpec((None, TATT, D_MODEL), lambda b, t: (b, t, 0))
    kv_spec = pl.BlockSpec((None, N_MEM, D_MODEL), lambda b, t: (b, 0, 0))
    qa_spec = pl.BlockSpec((ATT_ROWS * TS, D_MODEL), lambda b, t: (b * nta + t, 0))
    ckv_spec = pl.BlockSpec((ATT_ROWS, N_MEM, MEM_HEADS, MEM_DH), lambda b, t: (b * nta + t, 0, 0, 0))
    h2_p, ao_s = pl.pallas_call(
        _attn_kernel,
        grid=(B, nta),
        in_specs=[att_spec, _const_spec((1, D_MODEL)), _const_spec((D_MODEL, D_MODEL)),
                  _const_spec((D_MODEL, D_MODEL)), kv_spec, kv_spec, qa_spec, ckv_spec, ckv_spec],
        out_specs=[att_spec, qa_spec],
        out_shape=[jax.ShapeDtypeStruct((B, T, D_MODEL), F32), jax.ShapeDtypeStruct((ntok_s, D_MODEL), F32)],
        compiler_params=_params(("arbitrary", "arbitrary")),
        name="attn",
    )(h1_p, gx, wq_b, wo_b, mkb_p, mvb_p, q_s, cache_mem_k[0], cache_mem_v[0])

    big_spec = pl.BlockSpec((None, TBIG, D_MODEL), lambda b, t: (b, t, 0))
    ffn_w_specs = [_const_spec((1, D_MODEL)), _const_spec((D_MODEL, D_FF)), _const_spec((D_MODEL, D_FF)),
                   _const_spec((CONV_W, D_FF)), _const_spec((1, D_FF)), _const_spec((D_FF, D_MODEL)),
                   _const_spec((1, D_MODEL))]
    y_p, tail_p = pl.pallas_call(
        _p_ffn_kernel,
        grid=(B, T // TBIG),
        in_specs=[big_spec] + ffn_w_specs,
        out_specs=[big_spec, pl.BlockSpec((None, 8, D_FF), lambda b, t: (b, 0, 0))],
        out_shape=[jax.ShapeDtypeStruct((B, T, D_MODEL), F32), jax.ShapeDtypeStruct((B, 8, D_FF), F32)],
        scratch_shapes=[pltpu.VMEM((8, D_FF), F32), pltpu.VMEM((TBIG, D_FF), BF16)],
        compiler_params=_params(("arbitrary", "arbitrary")),
        name="p_ffn",
    )(h2_p, gffn, wg_b, wu_b, cw, cb, wd_b, gfin)

    hist_spec = pl.BlockSpec((rows_per_step, CONV_W - 1, D_FF), lambda i: (i, 0, 0))
    y_s, tail_s = pl.pallas_call(
        _s_ffn_kernel,
        grid=(ntok_s // TT,),
        in_specs=[tok_spec, tok_spec, _const_spec((D_MODEL, D_MODEL)), hist_spec] + ffn_w_specs,
        out_specs=[row3_spec, hist_spec],
        out_shape=[jax.ShapeDtypeStruct((RB, TS, D_MODEL), F32),
                   jax.ShapeDtypeStruct((RB, CONV_W - 1, D_FF), F32)],
        scratch_shapes=[pltpu.VMEM((TT, D_FF), BF16)],
        compiler_params=_params(("arbitrary",)),
        name="s_ffn",
    )(h1_s, ao_s, wo_b, state_conv[0], gffn, wg_b, wu_b, cw, cb, wd_b, gfin)

    return (y_p, y_s, s_p[None], tail_p[:, 6:8][None], mk_p[None], mv_p[None], s_s[None], tail_s[None], sv_s[None])
```

```python
import functools

import jax
import jax.numpy as jnp
from jax import lax
from jax.experimental import pallas as pl
from jax.experimental.pallas import tpu as pltpu

F32 = jnp.float32
BF16 = jnp.bfloat16

D_MODEL = 1024
GLA_HEADS = 4
GLA_DK = 64
GLA_DV = 128
QK_COLS = GLA_HEADS * GLA_DK
GLA_WIDTH = GLA_HEADS * GLA_DV
GLA_RANK = 16
GLA_TAU = 16.0
SGU_HEADS = 4
SGU_DH = 128
SGU_WIDTH = SGU_HEADS * SGU_DH
SGU_CHUNK = 128
N_MEM = 256
MEM_HEADS = 4
MEM_DH = 256
D_FF = 2816
CONV_W = 3
EPS = 1e-6

LANES = 128
RANK_PAD = LANES
TT = 256
TBIG = 1024
TMIX = 1024
GLA_BLOCK = 64
SROWS = 32
STOK = SROWS * 4
FF_CHUNK = 256
TATT = 512
ATT_ROWS = 4
VMEM_LIMIT = 56 * 1024 * 1024


def _dot(a, b):
    return jnp.dot(a.astype(BF16), b.astype(BF16), preferred_element_type=F32)


def _dot_nt(a, b):
    return lax.dot_general(a.astype(BF16), b.astype(BF16), (((1,), (1,)), ((), ())),
                           preferred_element_type=F32)


def _dot_tn(a, b):
    return lax.dot_general(a.astype(BF16), b.astype(BF16), (((0,), (0,)), ((), ())),
                           preferred_element_type=F32)


def _split3(x):
    hi = x.astype(BF16)
    r1 = x - hi.astype(F32)
    mid = r1.astype(BF16)
    lo = (r1 - mid.astype(F32)).astype(BF16)
    return hi, mid, lo


def _dot_exact_lhs(sel, parts):
    hi, mid, lo = parts
    return (jnp.dot(sel, hi, preferred_element_type=F32)
            + jnp.dot(sel, mid, preferred_element_type=F32)
            + jnp.dot(sel, lo, preferred_element_type=F32))


def _rms(x, g):
    ms = jnp.mean(x * x, axis=-1, keepdims=True)
    return x * lax.rsqrt(ms + EPS) * g


def _rms_heads(x, g, nh, dh):
    outs = []
    for h in range(nh):
        xh = x[:, h * dh:(h + 1) * dh]
        ms = jnp.mean(xh * xh, axis=-1, keepdims=True)
        outs.append(xh * lax.rsqrt(ms + EPS))
    return jnp.concatenate(outs, axis=-1) * g


def _gelu(x):
    c = 0.7978845608028654
    return x * (0.5 * (1.0 + jnp.tanh(c * (x + 0.044715 * (x * x * x)))))


def _silu(x):
    return x * (1.0 / (1.0 + jnp.exp(-x)))


def _log_sigmoid(x):
    return jnp.minimum(x, 0.0) - jnp.log1p(jnp.exp(-jnp.abs(x)))


def _iota(shape, dim):
    return lax.broadcasted_iota(jnp.int32, shape, dim)


def _in_proj(x, gmix, w_qkvra, w_usv, w_alpha, b_alpha):
    xn = _rms(x, gmix).astype(BF16)
    p = jnp.dot(xn, w_qkvra, preferred_element_type=F32)
    q = p[:, 0:256] * (GLA_DK ** -0.5)
    k = p[:, 256:512]
    v = p[:, 512:1024]
    r = p[:, 1024:1536]
    xg = _dot(p[:, 1536:1664], w_alpha) + b_alpha
    logg = _log_sigmoid(xg) * (1.0 / GLA_TAU)
    p2 = jnp.dot(xn, w_usv, preferred_element_type=F32)
    u = p2[:, 0:512]
    sv = p2[:, 512:1024]
    return q, k, v, r, u, sv, logg


def _mix_out(x, o, r, u, z, ggla, w_out):
    og = _rms_heads(o, ggla, GLA_HEADS, GLA_DV) * _silu(r)
    s_out = u * z
    y = _dot(og, w_out[0:GLA_WIDTH, :]) + _dot(s_out, w_out[GLA_WIDTH:, :])
    return x + y


def _p_mixer_kernel(x_ref, gmix_ref, wqkvra_ref, wusv_ref, walpha_ref, balpha_ref, ggla_ref, gsgu_ref,
                    ws_ref, bs_ref, wout_ref, h_ref, s_ref, st_ref):
    t = pl.program_id(1)

    @pl.when(t == 0)
    def _():
        st_ref[...] = jnp.zeros_like(st_ref)

    x = x_ref[...]
    n = TT
    subs = [slice(s * n, (s + 1) * n) for s in range(x.shape[0] // n)]
    nblk = n // GLA_BLOCK
    xn = _rms(x, gmix_ref[...]).astype(BF16)
    p = jnp.dot(xn, wqkvra_ref[...], preferred_element_type=F32)
    q = p[:, 0:256] * (GLA_DK ** -0.5)
    k = p[:, 256:512]
    v = p[:, 512:1024]
    r = p[:, 1024:1536]
    logg = _log_sigmoid(_dot(p[:, 1536:1664], walpha_ref[...]) + balpha_ref[...]) * (1.0 / GLA_TAU)

    ri = _iota((n, n), 0)
    ci = _iota((n, n), 1)
    low = (ci <= ri).astype(BF16)
    bts = [_dot_exact_lhs(low, _split3(logg[ts])) for ts in subs]
    p2 = jnp.dot(xn, wusv_ref[...], preferred_element_type=F32)

    headm = (ri >> 6) == (ci >> 6)
    scs = []
    for ts, bt in zip(subs, bts):
        qs_, ks_ = q[ts], k[ts]
        row = []
        for blk in range(nblk):
            r0 = blk * GLA_BLOCK
            if blk == 0:
                bq, ek = bt[0:GLA_BLOCK], jnp.exp(-bt)
            else:
                ref = bt[r0 - 1:r0, :]
                bq, ek = bt[r0:r0 + GLA_BLOCK] - ref, jnp.exp(ref - bt)
            qb = qs_[r0:r0 + GLA_BLOCK] * jnp.exp(bq)
            qstack = jnp.where(headm, jnp.concatenate([qb] * GLA_HEADS, axis=0), 0.0)
            row.append(_dot_nt(qstack, ks_ * ek))
        scs.append(row)

    u = _gelu(p2[:, 0:SGU_WIDTH])
    svn = _rms_heads(_gelu(p2[:, SGU_WIDTH:]), gsgu_ref[...], SGU_HEADS, SGU_DH)
    wm = ((ri >> 7) == (ci >> 7)) & (ci <= ri)
    ws = [jnp.where(wm, ws_ref[h].astype(F32), 0.0).astype(BF16) for h in range(SGU_HEADS)]
    zs = [jnp.concatenate([_dot(ws[h], svn[ts, h * SGU_DH:(h + 1) * SGU_DH]) for h in range(SGU_HEADS)], axis=1)
          + bs_ref[...] for ts in subs]

    cms = [_iota((GLA_BLOCK, n), 1) <= _iota((GLA_BLOCK, n), 0) + blk * GLA_BLOCK for blk in range(nblk)]
    o_intra = []
    for ts, row in zip(subs, scs):
        vs_ = v[ts]
        heads = []
        for h in range(GLA_HEADS):
            hs = slice(h * GLA_BLOCK, (h + 1) * GLA_BLOCK)
            sc_h = jnp.concatenate([jnp.where(cms[blk], row[blk][hs], 0.0) for blk in range(nblk)], axis=0)
            heads.append(_dot(sc_h, vs_[:, h * GLA_DV:(h + 1) * GLA_DV]))
        o_intra.append(jnp.concatenate(heads, axis=1))
    y_sgu = _dot(u * jnp.concatenate(zs, axis=0), wout_ref[GLA_WIDTH:, :])

    bdm = (_iota((GLA_WIDTH, QK_COLS), 0) >> 7) == (_iota((GLA_WIDTH, QK_COLS), 1) >> 6)
    upds = [_dot_tn(v[ts], k[ts] * jnp.exp(bt[n - 1:n, :] - bt)) for ts, bt in zip(subs, bts)]
    sts = [st_ref[...]]
    for bt, upd in zip(bts, upds):
        sts.append(jnp.exp(bt[n - 1:n, :]) * sts[-1] + jnp.where(bdm, upd, 0.0))
    st_ref[...] = sts[-1]
    o = jnp.concatenate([oi + _dot_nt(q[ts] * jnp.exp(bt), st)
                         for ts, bt, st, oi in zip(subs, bts, sts, o_intra)], axis=0)
    og = _rms_heads(o, ggla_ref[...], GLA_HEADS, GLA_DV) * _silu(r)
    h_ref[...] = x + (_dot(og, wout_ref[0:GLA_WIDTH, :]) + y_sgu)

    @pl.when(t == pl.num_programs(1) - 1)
    def _():
        s_full = st_ref[...].T
        for h in range(GLA_HEADS):
            s_ref[h] = s_full[h * GLA_DK:(h + 1) * GLA_DK, h * GLA_DV:(h + 1) * GLA_DV]


def _gla_sample_block(q, k, v, logg, sin_ref, sout_ref, row0):
    n = STOK
    ri = _iota((n, n), 0)
    ci = _iota((n, n), 1)
    same = (ri >> 2) == (ci >> 2)
    causal = same & (ci <= ri)
    parts = _split3(logg)
    b = _dot_exact_lhs(causal.astype(BF16), parts)
    bl = _dot_exact_lhs(same.astype(BF16), parts)
    qt = q * jnp.exp(b)
    kt = k * jnp.exp(-b)
    kh = k * jnp.exp(bl - b)
    headm = (_iota((GLA_HEADS * n, QK_COLS), 0) >> 7) == (_iota((GLA_HEADS * n, QK_COLS), 1) >> 6)
    qs = jnp.where(headm, jnp.concatenate([qt] * GLA_HEADS, axis=0), 0.0)
    sc = _dot_nt(qs, kt)
    tok = _iota((GLA_HEADS * n, n), 0) & (n - 1)
    col = _iota((GLA_HEADS * n, n), 1)
    sc = jnp.where(((tok >> 2) == (col >> 2)) & (col <= tok), sc, 0.0)
    ov = _dot(sc, v)
    o_intra = jnp.concatenate(
        [ov[h * n:(h + 1) * n, h * GLA_DV:(h + 1) * GLA_DV] for h in range(GLA_HEADS)], axis=1)

    nexp = SROWS * GLA_DK
    expand = ((_iota((GLA_DK, nexp), 1) & (GLA_DK - 1)) == _iota((GLA_DK, nexp), 0)).astype(BF16)
    expand_t = ((_iota((nexp, GLA_DK), 0) & (GLA_DK - 1)) == _iota((nexp, GLA_DK), 1)).astype(BF16)
    bd = (_iota((n, nexp), 0) >> 2) == (_iota((n, nexp), 1) >> 6)
    bd_t = (_iota((nexp, n), 0) >> 6) == (_iota((nexp, n), 1) >> 2)
    dec_t = jnp.exp(bl).T
    kh_t = kh.T
    outs = []
    for h in range(GLA_HEADS):
        dsl = slice(h * GLA_DK, (h + 1) * GLA_DK)
        s_in = sin_ref[row0:row0 + SROWS, h].reshape(nexp, GLA_DV)
        qe = jnp.where(bd, _dot(qt[:, dsl], expand), 0.0)
        outs.append(_dot(qe, s_in))
        ke_t = jnp.where(bd_t, _dot(expand_t, kh_t[dsl, :]), 0.0)
        upd = _dot(ke_t, v[:, h * GLA_DV:(h + 1) * GLA_DV])
        for rr in range(SROWS):
            dcol = jnp.broadcast_to(dec_t[dsl, 4 * rr:4 * rr + 1], (GLA_DK, GLA_DV))
            sout_ref[row0 + rr, h] = dcol * sin_ref[row0 + rr, h] + upd[rr * GLA_DK:(rr + 1) * GLA_DK]
    return o_intra + jnp.concatenate(outs, axis=1)


def _s_mixer_kernel(x_ref, sin_ref, gmix_ref, wqkvra_ref, wusv_ref, walpha_ref, balpha_ref, ggla_ref,
                    gsgu_ref, wsp_ref, bsp_ref, wout_ref, gx_ref, wq_ref,
                    h_ref, q_ref, sv_ref, sout_ref):
    x = x_ref[...].reshape(TT, D_MODEL)
    q, k, v, r, u, sv, logg = _in_proj(x, gmix_ref[...], wqkvra_ref[...], wusv_ref[...],
                                       walpha_ref[...], balpha_ref[...])
    u = _gelu(u)
    svn = _rms_heads(_gelu(sv), gsgu_ref[...], SGU_HEADS, SGU_DH)
    sv_ref[...] = svn.reshape(TT // 4, 4, SGU_HEADS, SGU_DH)
    z = jnp.tile(bsp_ref[...], (TT // 8, 1))
    for kk in range(4):
        shifted = svn if kk == 0 else pltpu.roll(svn, kk, 0)
        z = z + jnp.tile(wsp_ref[kk], (TT // 8, 1)) * shifted
    os_ = []
    for sb in range(TT // STOK):
        ts = slice(sb * STOK, (sb + 1) * STOK)
        os_.append(_gla_sample_block(q[ts], k[ts], v[ts], logg[ts], sin_ref, sout_ref, sb * SROWS))
    o = jnp.concatenate(os_, axis=0)
    h1 = _mix_out(x, o, r, u, z, ggla_ref[...], wout_ref)
    h_ref[...] = h1
    q_ref[...] = _dot(_rms(h1, gx_ref[...]), wq_ref[...])


def _softmax_rows(s):
    m = jnp.max(s, axis=-1, keepdims=True)
    e = jnp.exp(s - m)
    return e / jnp.sum(e, axis=-1, keepdims=True)


def _memkv_kernel(mem_ref, gmem_ref, wk_ref, wv_ref, mk_ref, mv_ref, mkb_ref, mvb_ref):
    mn = _rms(mem_ref[...], gmem_ref[...])
    mk = _dot(mn, wk_ref[...])
    mv = _dot(mn, wv_ref[...])
    mk_ref[...] = mk.reshape(N_MEM, MEM_HEADS, MEM_DH)
    mv_ref[...] = mv.reshape(N_MEM, MEM_HEADS, MEM_DH)
    mkb_ref[...] = mk.astype(BF16)
    mvb_ref[...] = mv.astype(BF16)


def _attn_kernel(h_ref, gx_ref, wq_ref, wo_ref, mk_ref, mv_ref, qs_ref, ck_ref, cv_ref, o_ref, aos_ref):
    nq = MEM_HEADS * 4
    nk = N_MEM * MEM_HEADS
    own_head = (_iota((nq, nk), 1) & (MEM_HEADS - 1)) == (_iota((nq, nk), 0) >> 2)
    h1 = h_ref[...]
    q = _dot(_rms(h1, gx_ref[...]), wq_ref[...])

    scores = []
    for r in range(ATT_ROWS):
        qr = qs_ref[4 * r:4 * r + 4, :]
        qh = jnp.concatenate([qr[:, h * MEM_DH:(h + 1) * MEM_DH] for h in range(MEM_HEADS)], axis=0)
        scores.append(_dot_nt(qh, ck_ref[r].reshape(nk, MEM_DH)))

    sl = [slice(h * MEM_DH, (h + 1) * MEM_DH) for h in range(MEM_HEADS)]
    ps = [_dot_nt(q[:, sl[h]], mk_ref[:, sl[h]]) for h in range(MEM_HEADS)]

    for r in range(ATT_ROWS):
        p = _softmax_rows(jnp.where(own_head, scores[r] * (MEM_DH ** -0.5), -jnp.inf))
        o = _dot(p, cv_ref[r].reshape(nk, MEM_DH))
        for h in range(MEM_HEADS):
            aos_ref[4 * r:4 * r + 4, h * MEM_DH:(h + 1) * MEM_DH] = o[4 * h:4 * h + 4]

    outs = [_dot(_softmax_rows(ps[h] * (MEM_DH ** -0.5)), mv_ref[:, sl[h]]) for h in range(MEM_HEADS)]
    o_ref[...] = h1 + _dot(jnp.concatenate(outs, axis=1), wo_ref[...])


def _ffn_chunks(hn, wg_ref, wu_ref, cw_ref, cb_ref, wd_ref, prod_ref, shifts, sink):
    hb = hn.astype(BF16)
    for c in range(D_FF // FF_CHUNK):
        cs = slice(c * FF_CHUNK, (c + 1) * FF_CHUNK)
        g = jnp.dot(hb, wg_ref[:, cs], preferred_element_type=F32)
        s1, s2 = shifts(g, cs)
        conv = cb_ref[:, cs] + ((cw_ref[0:1, cs] * s2 + cw_ref[1:2, cs] * s1) + cw_ref[2:3, cs] * g)
        up = jnp.dot(hb, wu_ref[:, cs], preferred_element_type=F32)
        prod_ref[:, cs] = (_gelu(conv) * up).astype(BF16)
        sink(g, cs)
    return jnp.dot(prod_ref[...], wd_ref[...], preferred_element_type=F32)


def _p_ffn_kernel(h_ref, gffn_ref, wg_ref, wu_ref, cw_ref, cb_ref, wd_ref, gfin_ref,
                  y_ref, tail_ref, carry_ref, prod_ref):
    t = pl.program_id(1)

    @pl.when(t == 0)
    def _():
        carry_ref[...] = jnp.zeros_like(carry_ref)

    h2 = h_ref[...]
    m = h2.shape[0]
    row = _iota((m, FF_CHUNK), 0)

    def shifts(g, cs):
        prev = carry_ref[:, cs]
        p1 = prev[7:8, :]
        p2 = prev[6:7, :]
        s1 = jnp.where(row == 0, p1, pltpu.roll(g, 1, 0))
        s2 = jnp.where(row == 0, p2, jnp.where(row == 1, p1, pltpu.roll(g, 2, 0)))
        return s1, s2

    def sink(g, cs):
        carry_ref[:, cs] = g[m - 8:m, :]

    f = _ffn_chunks(_rms(h2, gffn_ref[...]), wg_ref, wu_ref, cw_ref, cb_ref, wd_ref, prod_ref, shifts, sink)
    y_ref[...] = _rms(h2 + f, gfin_ref[...])

    @pl.when(t == pl.num_programs(1) - 1)
    def _():
        tail_ref[...] = carry_ref[...]


def _s_ffn_kernel(h_ref, ao_ref, wo_ref, buf_ref, gffn_ref, wg_ref, wu_ref, cw_ref, cb_ref,
                  wd_ref, gfin_ref, y_ref, tail_ref, prod_ref):
    h2 = h_ref[...] + _dot(ao_ref[...], wo_ref[...])
    nrow2 = TT // 2
    tpos = _iota((TT, FF_CHUNK), 0) & 3
    tok = _iota((TT, nrow2), 0)
    ent = _iota((TT, nrow2), 1)
    same_row = (tok >> 2) == (ent >> 1)
    sel2 = (same_row & ((tok & 3) == (ent & 1))).astype(BF16)
    sel1 = (same_row & ((tok & 3) == 0) & ((ent & 1) == 1)).astype(BF16)
    sel_tail = (((_iota((nrow2, TT), 1) >> 2) == (_iota((nrow2, TT), 0) >> 1))
                & ((_iota((nrow2, TT), 1) & 3) == (_iota((nrow2, TT), 0) & 1) + 2)).astype(BF16)

    def shifts(g, cs):
        parts = _split3(buf_ref[:, :, cs].reshape(nrow2, FF_CHUNK))
        s1 = jnp.where(tpos == 0, _dot_exact_lhs(sel1, parts), pltpu.roll(g, 1, 0))
        s2 = jnp.where(tpos < 2, _dot_exact_lhs(sel2, parts), pltpu.roll(g, 2, 0))
        return s1, s2

    def sink(g, cs):
        tail_ref[:, :, cs] = _dot_exact_lhs(sel_tail, _split3(g)).reshape(TT // 4, 2, FF_CHUNK)

    f = _ffn_chunks(_rms(h2, gffn_ref[...]), wg_ref, wu_ref, cw_ref, cb_ref, wd_ref, prod_ref, shifts, sink)
    y_ref[...] = _rms(h2 + f, gfin_ref[...]).reshape(TT // 4, 4, D_MODEL)


def _const_spec(shape, single=True):
    nd = len(shape)
    kw = {"pipeline_mode": pl.Buffered(1)} if single else {}
    return pl.BlockSpec(shape, lambda *_: (0,) * nd, **kw)


def _params(sem):
    return pltpu.CompilerParams(dimension_semantics=sem, vmem_limit_bytes=VMEM_LIMIT)


def kernel(x_prompt, x_sample, mem_prompt, state_gla, state_conv, cache_mem_k, cache_mem_v, g_mix, w_in, w_alpha, b_alpha, g_gla_out, g_sgu, w_s, b_s, w_out, g_x, g_mem, wq_x, wk_x, wv_x, wo_x, g_ffn, w_gate, w_up, conv_w, conv_b, w_down, g_final):
    B, T, _ = x_prompt.shape
    RB, TS, _ = x_sample.shape
    assert T % TMIX == 0 and TMIX % TT == 0 and T % TBIG == 0
    assert TS == 4 and (RB * TS) % TT == 0 and RB % ATT_ROWS == 0
    ntok_s = RB * TS
    rows_per_step = TT // TS

    w_in_b = w_in[0].astype(BF16)
    cut = 2 * QK_COLS + 2 * GLA_WIDTH
    w_qkvra = jnp.pad(w_in_b[:, :cut + GLA_RANK], ((0, 0), (0, RANK_PAD - GLA_RANK)))
    w_usv = w_in_b[:, cut + GLA_RANK:]
    w_alpha_p = jnp.concatenate([w_alpha[0], jnp.zeros((RANK_PAD - GLA_RANK, QK_COLS), F32)], axis=0).astype(BF16)
    b_alpha_r = b_alpha[0].reshape(1, QK_COLS)
    gmix = g_mix[0].reshape(1, D_MODEL)
    ggla = g_gla_out[0].reshape(1, GLA_WIDTH)
    gsgu = g_sgu[0].reshape(1, SGU_WIDTH)
    gx = g_x[0].reshape(1, D_MODEL)
    gmem = g_mem[0].reshape(1, D_MODEL)
    gffn = g_ffn[0].reshape(1, D_MODEL)
    gfin = g_final.reshape(1, D_MODEL)
    w_out_b = w_out[0].astype(BF16)
    wq_b, wk_b, wv_b, wo_b = (w[0].astype(BF16) for w in (wq_x, wk_x, wv_x, wo_x))
    wg_b, wu_b, wd_b = (w[0].astype(BF16) for w in (w_gate, w_up, w_down))
    cw = conv_w[0]
    cb = conv_b[0].reshape(1, D_FF)
    ws_p = jnp.tile(w_s[0], (1, TT // SGU_CHUNK, TT // SGU_CHUNK)).astype(BF16)
    bs_p = jnp.tile(jnp.repeat(b_s[0].T, SGU_DH, axis=1), (TT // SGU_CHUNK, 1))
    w4 = w_s[0][:, :TS, :TS]
    ws_s = jnp.stack([
        jnp.tile(jnp.repeat(jnp.pad(jnp.diagonal(w4, offset=-kk, axis1=1, axis2=2), ((0, 0), (kk, 0))).T,
                            SGU_DH, axis=1), (2, 1))
        for kk in range(TS)])
    bs_s = jnp.tile(jnp.repeat(b_s[0][:, :TS].T, SGU_DH, axis=1), (2, 1))

    mixer_w_specs = [
        _const_spec((1, D_MODEL)), _const_spec((D_MODEL, cut + RANK_PAD)), _const_spec((D_MODEL, 2 * SGU_WIDTH)),
        _const_spec((RANK_PAD, QK_COLS)),
        _const_spec((1, QK_COLS)), _const_spec((1, GLA_WIDTH)), _const_spec((1, SGU_WIDTH)),
    ]
    mixer_w = (gmix, w_qkvra, w_usv, w_alpha_p, b_alpha_r, ggla, gsgu)

    kv4_spec = pl.BlockSpec((None, N_MEM, MEM_HEADS, MEM_DH), lambda b: (b, 0, 0, 0))
    mk_p, mv_p, mkb_p, mvb_p = pl.pallas_call(
        _memkv_kernel,
        grid=(B,),
        in_specs=[pl.BlockSpec((None, N_MEM, D_MODEL), lambda b: (b, 0, 0)),
                  _const_spec((1, D_MODEL)), _const_spec((D_MODEL, D_MODEL)), _const_spec((D_MODEL, D_MODEL))],
        out_specs=[kv4_spec, kv4_spec] + [pl.BlockSpec((None, N_MEM, D_MODEL), lambda b: (b, 0, 0))] * 2,
        out_shape=[jax.ShapeDtypeStruct((B, N_MEM, MEM_HEADS, MEM_DH), F32)] * 2
        + [jax.ShapeDtypeStruct((B, N_MEM, D_MODEL), BF16)] * 2,
        compiler_params=_params(("arbitrary",)),
        name="memkv",
    )(mem_prompt, gmem, wk_b, wv_b)

    tile_spec = pl.BlockSpec((None, TMIX, D_MODEL), lambda b, t: (b, t, 0))
    h1_p, s_p = pl.pallas_call(
        _p_mixer_kernel,
        grid=(B, T // TMIX),
        in_specs=[tile_spec] + mixer_w_specs + [
            _const_spec((SGU_HEADS, TT, TT)), _const_spec((TT, SGU_WIDTH)), _const_spec((D_MODEL, D_MODEL))],
        out_specs=[tile_spec,
                   pl.BlockSpec((None, GLA_HEADS, GLA_DK, GLA_DV), lambda b, t: (b, 0, 0, 0))],
        out_shape=[jax.ShapeDtypeStruct((B, T, D_MODEL), F32),
                   jax.ShapeDtypeStruct((B, GLA_HEADS, GLA_DK, GLA_DV), F32)],
        scratch_shapes=[pltpu.VMEM((GLA_WIDTH, QK_COLS), F32)],
        compiler_params=_params(("arbitrary", "arbitrary")),
        name="p_mixer",
    )(x_prompt, *mixer_w, ws_p, bs_p, w_out_b)

    s_in = state_gla[0]
    row3_spec = pl.BlockSpec((rows_per_step, TS, D_MODEL), lambda i: (i, 0, 0))
    tok_spec = pl.BlockSpec((TT, D_MODEL), lambda i: (i, 0))
    st_spec = pl.BlockSpec((rows_per_step, GLA_HEADS, GLA_DK, GLA_DV), lambda i: (i, 0, 0, 0))
    h1_s, q_s, sv_s, s_s = pl.pallas_call(
        _s_mixer_kernel,
        grid=(ntok_s // TT,),
        in_specs=[row3_spec, st_spec] + mixer_w_specs + [
            _const_spec((TS, 8, SGU_WIDTH)), _const_spec((8, SGU_WIDTH)), _const_spec((D_MODEL, D_MODEL)),
            _const_spec((1, D_MODEL)), _const_spec((D_MODEL, D_MODEL))],
        out_specs=[tok_spec, tok_spec,
                   pl.BlockSpec((rows_per_step, TS, SGU_HEADS, SGU_DH), lambda i: (i, 0, 0, 0)), st_spec],
        out_shape=[jax.ShapeDtypeStruct((ntok_s, D_MODEL), F32), jax.ShapeDtypeStruct((ntok_s, D_MODEL), F32),
                   jax.ShapeDtypeStruct((RB, TS, SGU_HEADS, SGU_DH), F32),
                   jax.ShapeDtypeStruct((RB, GLA_HEADS, GLA_DK, GLA_DV), F32)],
        compiler_params=_params(("arbitrary",)),
        name="s_mixer",
    )(x_sample, s_in, *mixer_w, ws_s, bs_s, w_out_b, gx, wq_b)

    nta = T // TATT
    assert B * nta * ATT_ROWS == RB
    att_spec = pl.BlockSpec((None, TATT, D_MODEL), lambda b, t: (b, t, 0))
    kv_spec = pl.BlockSpec((None, N_MEM, D_MODEL), lambda b, t: (b, 0, 0))
    qa_spec = pl.BlockSpec((ATT_ROWS * TS, D_MODEL), lambda b, t: (b * nta + t, 0))
    ckv_spec = pl.BlockSpec((ATT_ROWS, N_MEM, MEM_HEADS, MEM_DH), lambda b, t: (b * nta + t, 0, 0, 0))
    h2_p, ao_s = pl.pallas_call(
        _attn_kernel,
        grid=(B, nta),
        in_specs=[att_spec, _const_spec((1, D_MODEL)), _const_spec((D_MODEL, D_MODEL)),
                  _const_spec((D_MODEL, D_MODEL)), kv_spec, kv_spec, qa_spec, ckv_spec, ckv_spec],
        out_specs=[att_spec, qa_spec],
        out_shape=[jax.ShapeDtypeStruct((B, T, D_MODEL), F32), jax.ShapeDtypeStruct((ntok_s, D_MODEL), F32)],
        compiler_params=_params(("arbitrary", "arbitrary")),
        name="attn",
    )(h1_p, gx, wq_b, wo_b, mkb_p, mvb_p, q_s, cache_mem_k[0], cache_mem_v[0])

    big_spec = pl.BlockSpec((None, TBIG, D_MODEL), lambda b, t: (b, t, 0))
    ffn_w_specs = [_const_spec((1, D_MODEL)), _const_spec((D_MODEL, D_FF)), _const_spec((D_MODEL, D_FF)),
                   _const_spec((CONV_W, D_FF)), _const_spec((1, D_FF)), _const_spec((D_FF, D_MODEL)),
                   _const_spec((1, D_MODEL))]
    y_p, tail_p = pl.pallas_call(
        _p_ffn_kernel,
        grid=(B, T // TBIG),
        in_specs=[big_spec] + ffn_w_specs,
        out_specs=[big_spec, pl.BlockSpec((None, 8, D_FF), lambda b, t: (b, 0, 0))],
        out_shape=[jax.ShapeDtypeStruct((B, T, D_MODEL), F32), jax.ShapeDtypeStruct((B, 8, D_FF), F32)],
        scratch_shapes=[pltpu.VMEM((8, D_FF), F32), pltpu.VMEM((TBIG, D_FF), BF16)],
        compiler_params=_params(("arbitrary", "arbitrary")),
        name="p_ffn",
    )(h2_p, gffn, wg_b, wu_b, cw, cb, wd_b, gfin)

    hist_spec = pl.BlockSpec((rows_per_step, CONV_W - 1, D_FF), lambda i: (i, 0, 0))
    y_s, tail_s = pl.pallas_call(
        _s_ffn_kernel,
        grid=(ntok_s // TT,),
        in_specs=[tok_spec, tok_spec, _const_spec((D_MODEL, D_MODEL)), hist_spec] + ffn_w_specs,
        out_specs=[row3_spec, hist_spec],
        out_shape=[jax.ShapeDtypeStruct((RB, TS, D_MODEL), F32),
                   jax.ShapeDtypeStruct((RB, CONV_W - 1, D_FF), F32)],
        scratch_shapes=[pltpu.VMEM((TT, D_FF), BF16)],
        compiler_params=_params(("arbitrary",)),
        name="s_ffn",
    )(h1_s, ao_s, wo_b, state_conv[0], gffn, wg_b, wu_b, cw, cb, wd_b, gfin)

    return (y_p, y_s, s_p[None], tail_p[:, 6:8][None], mk_p[None], mv_p[None], s_s[None], tail_s[None], sv_s[None])
```

```python
import functools

import jax
import jax.numpy as jnp
from jax import lax
from jax.experimental import pallas as pl
from jax.experimental.pallas import tpu as pltpu

F32 = jnp.float32
BF16 = jnp.bfloat16

D_MODEL = 1024
GLA_HEADS = 4
GLA_DK = 64
GLA_DV = 128
QK_COLS = GLA_HEADS * GLA_DK
GLA_WIDTH = GLA_HEADS * GLA_DV
GLA_RANK = 16
GLA_TAU = 16.0
SGU_HEADS = 4
SGU_DH = 128
SGU_WIDTH = SGU_HEADS * SGU_DH
SGU_CHUNK = 128
N_MEM = 256
MEM_HEADS = 4
MEM_DH = 256
D_FF = 2816
CONV_W = 3
EPS = 1e-6

LANES = 128
RANK_PAD = LANES
TT = 256
TBIG = 1024
TMIX = 1024
GLA_BLOCK = 64
SROWS = 32
STOK = SROWS * 4
FF_CHUNK = 256
TATT = 512
ATT_ROWS = 4
VMEM_LIMIT = 56 * 1024 * 1024


def _dot(a, b):
    return jnp.dot(a.astype(BF16), b.astype(BF16), preferred_element_type=F32)


def _dot_nt(a, b):
    return lax.dot_general(a.astype(BF16), b.astype(BF16), (((1,), (1,)), ((), ())),
                           preferred_element_type=F32)


def _dot_tn(a, b):
    return lax.dot_general(a.astype(BF16), b.astype(BF16), (((0,), (0,)), ((), ())),
                           preferred_element_type=F32)


def _split3(x):
    hi = x.astype(BF16)
    r1 = x - hi.astype(F32)
    mid = r1.astype(BF16)
    lo = (r1 - mid.astype(F32)).astype(BF16)
    return hi, mid, lo


def _dot_exact_lhs(sel, parts):
    hi, mid, lo = parts
    return (jnp.dot(sel, hi, preferred_element_type=F32)
            + jnp.dot(sel, mid, preferred_element_type=F32)
            + jnp.dot(sel, lo, preferred_element_type=F32))


def _rms(x, g):
    ms = jnp.mean(x * x, axis=-1, keepdims=True)
    return x * lax.rsqrt(ms + EPS) * g


def _rms_heads(x, g, nh, dh):
    outs = []
    for h in range(nh):
        xh = x[:, h * dh:(h + 1) * dh]
        ms = jnp.mean(xh * xh, axis=-1, keepdims=True)
        outs.append(xh * lax.rsqrt(ms + EPS))
    return jnp.concatenate(outs, axis=-1) * g


def _gelu(x):
    c = 0.7978845608028654
    return x * (0.5 * (1.0 + jnp.tanh(c * (x + 0.044715 * (x * x * x)))))


def _silu(x):
    return x * (1.0 / (1.0 + jnp.exp(-x)))


def _log_sigmoid(x):
    return jnp.minimum(x, 0.0) - jnp.log1p(jnp.exp(-jnp.abs(x)))


def _iota(shape, dim):
    return lax.broadcasted_iota(jnp.int32, shape, dim)


def _in_proj(x, gmix, w_qkvra, w_usv, w_alpha, b_alpha):
    xn = _rms(x, gmix).astype(BF16)
    p = jnp.dot(xn, w_qkvra, preferred_element_type=F32)
    q = p[:, 0:256] * (GLA_DK ** -0.5)
    k = p[:, 256:512]
    v = p[:, 512:1024]
    r = p[:, 1024:1536]
    xg = _dot(p[:, 1536:1664], w_alpha) + b_alpha
    logg = _log_sigmoid(xg) * (1.0 / GLA_TAU)
    p2 = jnp.dot(xn, w_usv, preferred_element_type=F32)
    u = p2[:, 0:512]
    sv = p2[:, 512:1024]
    return q, k, v, r, u, sv, logg


def _mix_out(x, o, r, u, z, ggla, w_out):
    og = _rms_heads(o, ggla, GLA_HEADS, GLA_DV) * _silu(r)
    s_out = u * z
    y = _dot(og, w_out[0:GLA_WIDTH, :]) + _dot(s_out, w_out[GLA_WIDTH:, :])
    return x + y


def _p_mixer_kernel(x_ref, gmix_ref, wqkvra_ref, wusv_ref, walpha_ref, balpha_ref, ggla_ref, gsgu_ref,
                    ws_ref, bs_ref, wout_ref, h_ref, s_ref, st_ref):
    t = pl.program_id(1)

    @pl.when(t == 0)
    def _():
        st_ref[...] = jnp.zeros_like(st_ref)

    x = x_ref[...]
    n = TT
    subs = [slice(s * n, (s + 1) * n) for s in range(x.shape[0] // n)]
    nblk = n // GLA_BLOCK
    xn = _rms(x, gmix_ref[...]).astype(BF16)
    p = jnp.dot(xn, wqkvra_ref[...], preferred_element_type=F32)
    q = p[:, 0:256] * (GLA_DK ** -0.5)
    k = p[:, 256:512]
    v = p[:, 512:1024]
    r = p[:, 1024:1536]
    logg = _log_sigmoid(_dot(p[:, 1536:1664], walpha_ref[...]) + balpha_ref[...]) * (1.0 / GLA_TAU)

    ri = _iota((n, n), 0)
    ci = _iota((n, n), 1)
    low = (ci <= ri).astype(BF16)
    bts = [_dot_exact_lhs(low, _split3(logg[ts])) for ts in subs]
    p2 = jnp.dot(xn, wusv_ref[...], preferred_element_type=F32)

    headm = (ri >> 6) == (ci >> 6)
    scs = []
    for ts, bt in zip(subs, bts):
        qs_, ks_ = q[ts], k[ts]
        row = []
        for blk in range(nblk):
            r0 = blk * GLA_BLOCK
            if blk == 0:
                bq, ek = bt[0:GLA_BLOCK], jnp.exp(-bt)
            else:
                ref = bt[r0 - 1:r0, :]
                bq, ek = bt[r0:r0 + GLA_BLOCK] - ref, jnp.exp(ref - bt)
            qb = qs_[r0:r0 + GLA_BLOCK] * jnp.exp(bq)
            qstack = jnp.where(headm, jnp.concatenate([qb] * GLA_HEADS, axis=0), 0.0)
            row.append(_dot_nt(qstack, ks_ * ek))
        scs.append(row)

    u = _gelu(p2[:, 0:SGU_WIDTH])
    svn = _rms_heads(_gelu(p2[:, SGU_WIDTH:]), gsgu_ref[...], SGU_HEADS, SGU_DH)
    wm = ((ri >> 7) == (ci >> 7)) & (ci <= ri)
    ws = [jnp.where(wm, ws_ref[h].astype(F32), 0.0).astype(BF16) for h in range(SGU_HEADS)]
    zs = [jnp.concatenate([_dot(ws[h], svn[ts, h * SGU_DH:(h + 1) * SGU_DH]) for h in range(SGU_HEADS)], axis=1)
          + bs_ref[...] for ts in subs]

    cms = [_iota((GLA_BLOCK, n), 1) <= _iota((GLA_BLOCK, n), 0) + blk * GLA_BLOCK for blk in range(nblk)]
    o_intra = []
    for ts, row in zip(subs, scs):
        vs_ = v[ts]
        heads = []
        for h in range(GLA_HEADS):
            hs = slice(h * GLA_BLOCK, (h + 1) * GLA_BLOCK)
            sc_h = jnp.concatenate([jnp.where(cms[blk], row[blk][hs], 0.0) for blk in range(nblk)], axis=0)
            heads.append(_dot(sc_h, vs_[:, h * GLA_DV:(h + 1) * GLA_DV]))
        o_intra.append(jnp.concatenate(heads, axis=1))
    y_sgu = _dot(u * jnp.concatenate(zs, axis=0), wout_ref[GLA_WIDTH:, :])

    bdm = (_iota((GLA_WIDTH, QK_COLS), 0) >> 7) == (_iota((GLA_WIDTH, QK_COLS), 1) >> 6)
    upds = [_dot_tn(v[ts], k[ts] * jnp.exp(bt[n - 1:n, :] - bt)) for ts, bt in zip(subs, bts)]
    sts = [st_ref[...]]
    for bt, upd in zip(bts, upds):
        sts.append(jnp.exp(bt[n - 1:n, :]) * sts[-1] + jnp.where(bdm, upd, 0.0))
    st_ref[...] = sts[-1]
    o = jnp.concatenate([oi + _dot_nt(q[ts] * jnp.exp(bt), st)
                         for ts, bt, st, oi in zip(subs, bts, sts, o_intra)], axis=0)
    og = _rms_heads(o, ggla_ref[...], GLA_HEADS, GLA_DV) * _silu(r)
    h_ref[...] = x + (_dot(og, wout_ref[0:GLA_WIDTH, :]) + y_sgu)

    @pl.when(t == pl.num_programs(1) - 1)
    def _():
        s_full = st_ref[...].T
        for h in range(GLA_HEADS):
            s_ref[h] = s_full[h * GLA_DK:(h + 1) * GLA_DK, h * GLA_DV:(h + 1) * GLA_DV]


def _gla_sample_block(q, k, v, logg, sin_ref, sout_ref, row0):
    n = STOK
    ri = _iota((n, n), 0)
    ci = _iota((n, n), 1)
    same = (ri >> 2) == (ci >> 2)
    causal = same & (ci <= ri)
    parts = _split3(logg)
    b = _dot_exact_lhs(causal.astype(BF16), parts)
    bl = _dot_exact_lhs(same.astype(BF16), parts)
    qt = q * jnp.exp(b)
    kt = k * jnp.exp(-b)
    kh = k * jnp.exp(bl - b)
    headm = (_iota((GLA_HEADS * n, QK_COLS), 0) >> 7) == (_iota((GLA_HEADS * n, QK_COLS), 1) >> 6)
    qs = jnp.where(headm, jnp.concatenate([qt] * GLA_HEADS, axis=0), 0.0)
    sc = _dot_nt(qs, kt)
    tok = _iota((GLA_HEADS * n, n), 0) & (n - 1)
    col = _iota((GLA_HEADS * n, n), 1)
    sc = jnp.where(((tok >> 2) == (col >> 2)) & (col <= tok), sc, 0.0)
    ov = _dot(sc, v)
    o_intra = jnp.concatenate(
        [ov[h * n:(h + 1) * n, h * GLA_DV:(h + 1) * GLA_DV] for h in range(GLA_HEADS)], axis=1)

    nexp = SROWS * GLA_DK
    expand = ((_iota((GLA_DK, nexp), 1) & (GLA_DK - 1)) == _iota((GLA_DK, nexp), 0)).astype(BF16)
    expand_t = ((_iota((nexp, GLA_DK), 0) & (GLA_DK - 1)) == _iota((nexp, GLA_DK), 1)).astype(BF16)
    bd = (_iota((n, nexp), 0) >> 2) == (_iota((n, nexp), 1) >> 6)
    bd_t = (_iota((nexp, n), 0) >> 6) == (_iota((nexp, n), 1) >> 2)
    dec_t = jnp.exp(bl).T
    kh_t = kh.T
    outs = []
    for h in range(GLA_HEADS):
        dsl = slice(h * GLA_DK, (h + 1) * GLA_DK)
        s_in = sin_ref[row0:row0 + SROWS, h].reshape(nexp, GLA_DV)
        qe = jnp.where(bd, _dot(qt[:, dsl], expand), 0.0)
        outs.append(_dot(qe, s_in))
        ke_t = jnp.where(bd_t, _dot(expand_t, kh_t[dsl, :]), 0.0)
        upd = _dot(ke_t, v[:, h * GLA_DV:(h + 1) * GLA_DV])
        for rr in range(SROWS):
            dcol = jnp.broadcast_to(dec_t[dsl, 4 * rr:4 * rr + 1], (GLA_DK, GLA_DV))
            sout_ref[row0 + rr, h] = dcol * sin_ref[row0 + rr, h] + upd[rr * GLA_DK:(rr + 1) * GLA_DK]
    return o_intra + jnp.concatenate(outs, axis=1)


def _s_mixer_kernel(x_ref, sin_ref, gmix_ref, wqkvra_ref, wusv_ref, walpha_ref, balpha_ref, ggla_ref,
                    gsgu_ref, wsp_ref, bsp_ref, wout_ref, gx_ref, wq_ref,
                    h_ref, q_ref, sv_ref, sout_ref):
    x = x_ref[...].reshape(TT, D_MODEL)
    q, k, v, r, u, sv, logg = _in_proj(x, gmix_ref[...], wqkvra_ref[...], wusv_ref[...],
                                       walpha_ref[...], balpha_ref[...])
    u = _gelu(u)
    svn = _rms_heads(_gelu(sv), gsgu_ref[...], SGU_HEADS, SGU_DH)
    sv_ref[...] = svn.reshape(TT // 4, 4, SGU_HEADS, SGU_DH)
    z = jnp.tile(bsp_ref[...], (TT // 8, 1))
    for kk in range(4):
        shifted = svn if kk == 0 else pltpu.roll(svn, kk, 0)
        z = z + jnp.tile(wsp_ref[kk], (TT // 8, 1)) * shifted
    os_ = []
    for sb in range(TT // STOK):
        ts = slice(sb * STOK, (sb + 1) * STOK)
        os_.append(_gla_sample_block(q[ts], k[ts], v[ts], logg[ts], sin_ref, sout_ref, sb * SROWS))
    o = jnp.concatenate(os_, axis=0)
    h1 = _mix_out(x, o, r, u, z, ggla_ref[...], wout_ref)
    h_ref[...] = h1
    q_ref[...] = _dot(_rms(h1, gx_ref[...]), wq_ref[...])


def _softmax_rows(s):
    m = jnp.max(s, axis=-1, keepdims=True)
    e = jnp.exp(s - m)
    return e / jnp.sum(e, axis=-1, keepdims=True)


def _memkv_kernel(mem_ref, gmem_ref, wk_ref, wv_ref, mk_ref, mv_ref, mkb_ref, mvb_ref):
    mn = _rms(mem_ref[...], gmem_ref[...])
    mk = _dot(mn, wk_ref[...])
    mv = _dot(mn, wv_ref[...])
    mk_ref[...] = mk.reshape(N_MEM, MEM_HEADS, MEM_DH)
    mv_ref[...] = mv.reshape(N_MEM, MEM_HEADS, MEM_DH)
    mkb_ref[...] = mk.astype(BF16)
    mvb_ref[...] = mv.astype(BF16)


def _attn_kernel(h_ref, gx_ref, wq_ref, wo_ref, mk_ref, mv_ref, qs_ref, ck_ref, cv_ref, o_ref, aos_ref):
    nq = MEM_HEADS * 4
    nk = N_MEM * MEM_HEADS
    own_head = (_iota((nq, nk), 1) & (MEM_HEADS - 1)) == (_iota((nq, nk), 0) >> 2)
    h1 = h_ref[...]
    q = _dot(_rms(h1, gx_ref[...]), wq_ref[...])

    scores = []
    for r in range(ATT_ROWS):
        qr = qs_ref[4 * r:4 * r + 4, :]
        qh = jnp.concatenate([qr[:, h * MEM_DH:(h + 1) * MEM_DH] for h in range(MEM_HEADS)], axis=0)
        scores.append(_dot_nt(qh, ck_ref[r].reshape(nk, MEM_DH)))

    sl = [slice(h * MEM_DH, (h + 1) * MEM_DH) for h in range(MEM_HEADS)]
    ps = [_dot_nt(q[:, sl[h]], mk_ref[:, sl[h]]) for h in range(MEM_HEADS)]

    for r in range(ATT_ROWS):
        p = _softmax_rows(jnp.where(own_head, scores[r] * (MEM_DH ** -0.5), -jnp.inf))
        o = _dot(p, cv_ref[r].reshape(nk, MEM_DH))
        for h in range(MEM_HEADS):
            aos_ref[4 * r:4 * r + 4, h * MEM_DH:(h + 1) * MEM_DH] = o[4 * h:4 * h + 4]

    outs = [_dot(_softmax_rows(ps[h] * (MEM_DH ** -0.5)), mv_ref[:, sl[h]]) for h in range(MEM_HEADS)]
    o_ref[...] = h1 + _dot(jnp.concatenate(outs, axis=1), wo_ref[...])


def _ffn_chunks(hn, wg_ref, wu_ref, cw_ref, cb_ref, wd_ref, prod_ref, shifts, sink):
    hb = hn.astype(BF16)
    for c in range(D_FF // FF_CHUNK):
        cs = slice(c * FF_CHUNK, (c + 1) * FF_CHUNK)
        g = jnp.dot(hb, wg_ref[:, cs], preferred_element_type=F32)
        s1, s2 = shifts(g, cs)
        conv = cb_ref[:, cs] + ((cw_ref[0:1, cs] * s2 + cw_ref[1:2, cs] * s1) + cw_ref[2:3, cs] * g)
        up = jnp.dot(hb, wu_ref[:, cs], preferred_element_type=F32)
        prod_ref[:, cs] = (_gelu(conv) * up).astype(BF16)
        sink(g, cs)
    return jnp.dot(prod_ref[...], wd_ref[...], preferred_element_type=F32)


def _p_ffn_kernel(h_ref, gffn_ref, wg_ref, wu_ref, cw_ref, cb_ref, wd_ref, gfin_ref,
                  y_ref, tail_ref, carry_ref, prod_ref):
    t = pl.program_id(1)

    @pl.when(t == 0)
    def _():
        carry_ref[...] = jnp.zeros_like(carry_ref)

    h2 = h_ref[...]
    m = h2.shape[0]
    row = _iota((m, FF_CHUNK), 0)

    def shifts(g, cs):
        prev = carry_ref[:, cs]
        p1 = prev[7:8, :]
        p2 = prev[6:7, :]
        s1 = jnp.where(row == 0, p1, pltpu.roll(g, 1, 0))
        s2 = jnp.where(row == 0, p2, jnp.where(row == 1, p1, pltpu.roll(g, 2, 0)))
        return s1, s2

    def sink(g, cs):
        carry_ref[:, cs] = g[m - 8:m, :]

    f = _ffn_chunks(_rms(h2, gffn_ref[...]), wg_ref, wu_ref, cw_ref, cb_ref, wd_ref, prod_ref, shifts, sink)
    y_ref[...] = _rms(h2 + f, gfin_ref[...])

    @pl.when(t == pl.num_programs(1) - 1)
    def _():
        tail_ref[...] = carry_ref[...]


def _s_ffn_kernel(h_ref, ao_ref, wo_ref, buf_ref, gffn_ref, wg_ref, wu_ref, cw_ref, cb_ref,
                  wd_ref, gfin_ref, y_ref, tail_ref, prod_ref):
    h2 = h_ref[...] + _dot(ao_ref[...], wo_ref[...])
    nrow2 = TT // 2
    tpos = _iota((TT, FF_CHUNK), 0) & 3
    tok = _iota((TT, nrow2), 0)
    ent = _iota((TT, nrow2), 1)
    same_row = (tok >> 2) == (ent >> 1)
    sel2 = (same_row & ((tok & 3) == (ent & 1))).astype(BF16)
    sel1 = (same_row & ((tok & 3) == 0) & ((ent & 1) == 1)).astype(BF16)
    sel_tail = (((_iota((nrow2, TT), 1) >> 2) == (_iota((nrow2, TT), 0) >> 1))
                & ((_iota((nrow2, TT), 1) & 3) == (_iota((nrow2, TT), 0) & 1) + 2)).astype(BF16)

    def shifts(g, cs):
        parts = _split3(buf_ref[:, :, cs].reshape(nrow2, FF_CHUNK))
        s1 = jnp.where(tpos == 0, _dot_exact_lhs(sel1, parts), pltpu.roll(g, 1, 0))
        s2 = jnp.where(tpos < 2, _dot_exact_lhs(sel2, parts), pltpu.roll(g, 2, 0))
        return s1, s2

    def sink(g, cs):
        tail_ref[:, :, cs] = _dot_exact_lhs(sel_tail, _split3(g)).reshape(TT // 4, 2, FF_CHUNK)

    f = _ffn_chunks(_rms(h2, gffn_ref[...]), wg_ref, wu_ref, cw_ref, cb_ref, wd_ref, prod_ref, shifts, sink)
    y_ref[...] = _rms(h2 + f, gfin_ref[...]).reshape(TT // 4, 4, D_MODEL)


WCAST_STEPS = 8


def _wcast_kernel(win_ref, wout_ref, wq_ref, wk_ref, wv_ref, wo_ref, wg_ref, wu_ref, wd_ref,
                  oqkvra_ref, ousv_ref, oout_ref, oq_ref, ok_ref, ov_ref, oo_ref, og_ref, ou_ref, od_ref):
    cut = 2 * QK_COLS + 2 * GLA_WIDTH
    a_tile = win_ref[:, cut:cut + RANK_PAD]
    a_tile = jnp.where(_iota(a_tile.shape, 1) < GLA_RANK, a_tile, 0.0)
    oqkvra_ref[...] = jnp.concatenate([win_ref[:, 0:cut], a_tile], axis=1).astype(BF16)
    ousv_ref[...] = win_ref[:, cut + GLA_RANK:].astype(BF16)
    for src, dst in ((wout_ref, oout_ref), (wq_ref, oq_ref), (wk_ref, ok_ref), (wv_ref, ov_ref), (wo_ref, oo_ref),
                     (wg_ref, og_ref), (wu_ref, ou_ref), (wd_ref, od_ref)):
        dst[...] = src[...].astype(BF16)


def _const_spec(shape, single=True):
    nd = len(shape)
    kw = {"pipeline_mode": pl.Buffered(1)} if single else {}
    return pl.BlockSpec(shape, lambda *_: (0,) * nd, **kw)


def _params(sem):
    return pltpu.CompilerParams(dimension_semantics=sem, vmem_limit_bytes=VMEM_LIMIT)


def kernel(x_prompt, x_sample, mem_prompt, state_gla, state_conv, cache_mem_k, cache_mem_v, g_mix, w_in, w_alpha, b_alpha, g_gla_out, g_sgu, w_s, b_s, w_out, g_x, g_mem, wq_x, wk_x, wv_x, wo_x, g_ffn, w_gate, w_up, conv_w, conv_b, w_down, g_final):
    B, T, _ = x_prompt.shape
    RB, TS, _ = x_sample.shape
    assert T % TMIX == 0 and TMIX % TT == 0 and T % TBIG == 0
    assert TS == 4 and (RB * TS) % TT == 0 and RB % ATT_ROWS == 0
    ntok_s = RB * TS
    rows_per_step = TT // TS

    cut = 2 * QK_COLS + 2 * GLA_WIDTH
    d_in = w_in.shape[-1]
    rb = D_MODEL // WCAST_STEPS
    rbd = D_FF // WCAST_STEPS

    def _rows3(nrow, ncol):
        return pl.BlockSpec((None, nrow, ncol), lambda i: (0, i, 0))

    def _rows2(nrow, ncol):
        return pl.BlockSpec((nrow, ncol), lambda i: (i, 0))

    (w_qkvra, w_usv, w_out_b, wq_b, wk_b, wv_b, wo_b, wg_b, wu_b, wd_b) = pl.pallas_call(
        _wcast_kernel,
        grid=(WCAST_STEPS,),
        in_specs=[_rows3(rb, d_in)] + [_rows3(rb, D_MODEL)] * 5 + [_rows3(rb, D_FF)] * 2 + [_rows3(rbd, D_MODEL)],
        out_specs=[_rows2(rb, cut + RANK_PAD), _rows2(rb, 2 * SGU_WIDTH)] + [_rows2(rb, D_MODEL)] * 5
        + [_rows2(rb, D_FF)] * 2 + [_rows2(rbd, D_MODEL)],
        out_shape=[jax.ShapeDtypeStruct((D_MODEL, cut + RANK_PAD), BF16),
                   jax.ShapeDtypeStruct((D_MODEL, 2 * SGU_WIDTH), BF16)]
        + [jax.ShapeDtypeStruct((D_MODEL, D_MODEL), BF16)] * 5
        + [jax.ShapeDtypeStruct((D_MODEL, D_FF), BF16)] * 2 + [jax.ShapeDtypeStruct((D_FF, D_MODEL), BF16)],
        compiler_params=_params(("arbitrary",)),
        name="wcast",
    )(w_in, w_out, wq_x, wk_x, wv_x, wo_x, w_gate, w_up, w_down)

    w_alpha_p = jnp.concatenate([w_alpha[0], jnp.zeros((RANK_PAD - GLA_RANK, QK_COLS), F32)], axis=0).astype(BF16)
    b_alpha_r = b_alpha[0].reshape(1, QK_COLS)
    gmix = g_mix[0].reshape(1, D_MODEL)
    ggla = g_gla_out[0].reshape(1, GLA_WIDTH)
    gsgu = g_sgu[0].reshape(1, SGU_WIDTH)
    gx = g_x[0].reshape(1, D_MODEL)
    gmem = g_mem[0].reshape(1, D_MODEL)
    gffn = g_ffn[0].reshape(1, D_MODEL)
    gfin = g_final.reshape(1, D_MODEL)
    cw = conv_w[0]
    cb = conv_b[0].reshape(1, D_FF)
    ws_p = jnp.tile(w_s[0], (1, TT // SGU_CHUNK, TT // SGU_CHUNK)).astype(BF16)
    bs_p = jnp.tile(jnp.repeat(b_s[0].T, SGU_DH, axis=1), (TT // SGU_CHUNK, 1))
    w4 = w_s[0][:, :TS, :TS]
    ws_s = jnp.stack([
        jnp.tile(jnp.repeat(jnp.pad(jnp.diagonal(w4, offset=-kk, axis1=1, axis2=2), ((0, 0), (kk, 0))).T,
                            SGU_DH, axis=1), (2, 1))
        for kk in range(TS)])
    bs_s = jnp.tile(jnp.repeat(b_s[0][:, :TS].T, SGU_DH, axis=1), (2, 1))

    mixer_w_specs = [
        _const_spec((1, D_MODEL)), _const_spec((D_MODEL, cut + RANK_PAD)), _const_spec((D_MODEL, 2 * SGU_WIDTH)),
        _const_spec((RANK_PAD, QK_COLS)),
        _const_spec((1, QK_COLS)), _const_spec((1, GLA_WIDTH)), _const_spec((1, SGU_WIDTH)),
    ]
    mixer_w = (gmix, w_qkvra, w_usv, w_alpha_p, b_alpha_r, ggla, gsgu)

    kv4_spec = pl.BlockSpec((None, N_MEM, MEM_HEADS, MEM_DH), lambda b: (b, 0, 0, 0))
    mk_p, mv_p, mkb_p, mvb_p = pl.pallas_call(
        _memkv_kernel,
        grid=(B,),
        in_specs=[pl.BlockSpec((None, N_MEM, D_MODEL), lambda b: (b, 0, 0)),
                  _const_spec((1, D_MODEL)), _const_spec((D_MODEL, D_MODEL)), _const_spec((D_MODEL, D_MODEL))],
        out_specs=[kv4_spec, kv4_spec] + [pl.BlockSpec((None, N_MEM, D_MODEL), lambda b: (b, 0, 0))] * 2,
        out_shape=[jax.ShapeDtypeStruct((B, N_MEM, MEM_HEADS, MEM_DH), F32)] * 2
        + [jax.ShapeDtypeStruct((B, N_MEM, D_MODEL), BF16)] * 2,
        compiler_params=_params(("arbitrary",)),
        name="memkv",
    )(mem_prompt, gmem, wk_b, wv_b)

    tile_spec = pl.BlockSpec((None, TMIX, D_MODEL), lambda b, t: (b, t, 0))
    h1_p, s_p = pl.pallas_call(
        _p_mixer_kernel,
        grid=(B, T // TMIX),
        in_specs=[tile_spec] + mixer_w_specs + [
            _const_spec((SGU_HEADS, TT, TT)), _const_spec((TT, SGU_WIDTH)), _const_spec((D_MODEL, D_MODEL))],
        out_specs=[tile_spec,
                   pl.BlockSpec((None, GLA_HEADS, GLA_DK, GLA_DV), lambda b, t: (b, 0, 0, 0))],
        out_shape=[jax.ShapeDtypeStruct((B, T, D_MODEL), F32),
                   jax.ShapeDtypeStruct((B, GLA_HEADS, GLA_DK, GLA_DV), F32)],
        scratch_shapes=[pltpu.VMEM((GLA_WIDTH, QK_COLS), F32)],
        compiler_params=_params(("arbitrary", "arbitrary")),
        name="p_mixer",
    )(x_prompt, *mixer_w, ws_p, bs_p, w_out_b)

    s_in = state_gla[0]
    row3_spec = pl.BlockSpec((rows_per_step, TS, D_MODEL), lambda i: (i, 0, 0))
    tok_spec = pl.BlockSpec((TT, D_MODEL), lambda i: (i, 0))
    st_spec = pl.BlockSpec((rows_per_step, GLA_HEADS, GLA_DK, GLA_DV), lambda i: (i, 0, 0, 0))
    h1_s, q_s, sv_s, s_s = pl.pallas_call(
        _s_mixer_kernel,
        grid=(ntok_s // TT,),
        in_specs=[row3_spec, st_spec] + mixer_w_specs + [
            _const_spec((TS, 8, SGU_WIDTH)), _const_spec((8, SGU_WIDTH)), _const_spec((D_MODEL, D_MODEL)),
            _const_spec((1, D_MODEL)), _const_spec((D_MODEL, D_MODEL))],
        out_specs=[tok_spec, tok_spec,
                   pl.BlockSpec((rows_per_step, TS, SGU_HEADS, SGU_DH), lambda i: (i, 0, 0, 0)), st_spec],
        out_shape=[jax.ShapeDtypeStruct((ntok_s, D_MODEL), F32), jax.ShapeDtypeStruct((ntok_s, D_MODEL), F32),
                   jax.ShapeDtypeStruct((RB, TS, SGU_HEADS, SGU_DH), F32),
                   jax.ShapeDtypeStruct((RB, GLA_HEADS, GLA_DK, GLA_DV), F32)],
        compiler_params=_params(("arbitrary",)),
        name="s_mixer",
    )(x_sample, s_in, *mixer_w, ws_s, bs_s, w_out_b, gx, wq_b)

    nta = T // TATT
    assert B * nta * ATT_ROWS == RB
    att_spec = pl.BlockSpec((None, TATT, D_MODEL), lambda b, t: (b, t, 0))
    kv_spec = pl.BlockSpec((None, N_MEM, D_MODEL), lambda b, t: (b, 0, 0))
    qa_spec = pl.BlockSpec((ATT_ROWS * TS, D_MODEL), lambda b, t: (b * nta + t, 0))
    ckv_spec = pl.BlockSpec((ATT_ROWS, N_MEM, MEM_HEADS, MEM_DH), lambda b, t: (b * nta + t, 0, 0, 0))
    h2_p, ao_s = pl.pallas_call(
        _attn_kernel,
        grid=(B, nta),
        in_specs=[att_spec, _const_spec((1, D_MODEL)), _const_spec((D_MODEL, D_MODEL)),
                  _const_spec((D_MODEL, D_MODEL)), kv_spec, kv_spec, qa_spec, ckv_spec, ckv_spec],
        out_specs=[att_spec, qa_spec],
        out_shape=[jax.ShapeDtypeStruct((B, T, D_MODEL), F32), jax.ShapeDtypeStruct((ntok_s, D_MODEL), F32)],
        compiler_params=_params(("arbitrary", "arbitrary")),
        name="attn",
    )(h1_p, gx, wq_b, wo_b, mkb_p, mvb_p, q_s, cache_mem_k[0], cache_mem_v[0])

    big_spec = pl.BlockSpec((None, TBIG, D_MODEL), lambda b, t: (b, t, 0))
    ffn_w_specs = [_const_spec((1, D_MODEL)), _const_spec((D_MODEL, D_FF)), _const_spec((D_MODEL, D_FF)),
                   _const_spec((CONV_W, D_FF)), _const_spec((1, D_FF)), _const_spec((D_FF, D_MODEL)),
                   _const_spec((1, D_MODEL))]
    y_p, tail_p = pl.pallas_call(
        _p_ffn_kernel,
        grid=(B, T // TBIG),
        in_specs=[big_spec] + ffn_w_specs,
        out_specs=[big_spec, pl.BlockSpec((None, 8, D_FF), lambda b, t: (b, 0, 0))],
        out_shape=[jax.ShapeDtypeStruct((B, T, D_MODEL), F32), jax.ShapeDtypeStruct((B, 8, D_FF), F32)],
        scratch_shapes=[pltpu.VMEM((8, D_FF), F32), pltpu.VMEM((TBIG, D_FF), BF16)],
        compiler_params=_params(("arbitrary", "arbitrary")),
        name="p_ffn",
    )(h2_p, gffn, wg_b, wu_b, cw, cb, wd_b, gfin)

    hist_spec = pl.BlockSpec((rows_per_step, CONV_W - 1, D_FF), lambda i: (i, 0, 0))
    y_s, tail_s = pl.pallas_call(
        _s_ffn_kernel,
        grid=(ntok_s // TT,),
        in_specs=[tok_spec, tok_spec, _const_spec((D_MODEL, D_MODEL)), hist_spec] + ffn_w_specs,
        out_specs=[row3_spec, hist_spec],
        out_shape=[jax.ShapeDtypeStruct((RB, TS, D_MODEL), F32),
                   jax.ShapeDtypeStruct((RB, CONV_W - 1, D_FF), F32)],
        scratch_shapes=[pltpu.VMEM((TT, D_FF), BF16)],
        compiler_params=_params(("arbitrary",)),
        name="s_ffn",
    )(h1_s, ao_s, wo_b, state_conv[0], gffn, wg_b, wu_b, cw, cb, wd_b, gfin)

    return (y_p, y_s, s_p[None], tail_p[:, 6:8][None], mk_p[None], mv_p[None], s_s[None], tail_s[None], sv_s[None])
```

```python
import functools

import jax
import jax.numpy as jnp
from jax import lax
from jax.experimental import pallas as pl
from jax.experimental.pallas import tpu as pltpu

F32 = jnp.float32
BF16 = jnp.bfloat16

D_MODEL = 1024
GLA_HEADS = 4
GLA_DK = 64
GLA_DV = 128
QK_COLS = GLA_HEADS * GLA_DK
GLA_WIDTH = GLA_HEADS * GLA_DV
GLA_RANK = 16
GLA_TAU = 16.0
SGU_HEADS = 4
SGU_DH = 128
SGU_WIDTH = SGU_HEADS * SGU_DH
SGU_CHUNK = 128
N_MEM = 256
MEM_HEADS = 4
MEM_DH = 256
D_FF = 2816
CONV_W = 3
EPS = 1e-6

LANES = 128
RANK_PAD = LANES
TT = 256
TBIG = 1024
TMIX = 1024
GLA_BLOCK = 64
SROWS = 32
STOK = SROWS * 4
FF_CHUNK = 256
TATT = 512
ATT_ROWS = 4
MEMKV_ROWS = 4
VMEM_LIMIT = 56 * 1024 * 1024


def _dot(a, b):
    return jnp.dot(a.astype(BF16), b.astype(BF16), preferred_element_type=F32)


def _dot_nt(a, b):
    return lax.dot_general(a.astype(BF16), b.astype(BF16), (((1,), (1,)), ((), ())),
                           preferred_element_type=F32)


def _dot_tn(a, b):
    return lax.dot_general(a.astype(BF16), b.astype(BF16), (((0,), (0,)), ((), ())),
                           preferred_element_type=F32)


def _split3(x):
    hi = x.astype(BF16)
    r1 = x - hi.astype(F32)
    mid = r1.astype(BF16)
    lo = (r1 - mid.astype(F32)).astype(BF16)
    return hi, mid, lo


def _dot_exact_lhs(sel, parts):
    hi, mid, lo = parts
    return (jnp.dot(sel, hi, preferred_element_type=F32)
            + jnp.dot(sel, mid, preferred_element_type=F32)
            + jnp.dot(sel, lo, preferred_element_type=F32))


def _rms(x, g):
    ms = jnp.mean(x * x, axis=-1, keepdims=True)
    return x * lax.rsqrt(ms + EPS) * g


def _rms_heads(x, g, nh, dh):
    outs = []
    for h in range(nh):
        xh = x[:, h * dh:(h + 1) * dh]
        ms = jnp.mean(xh * xh, axis=-1, keepdims=True)
        outs.append(xh * lax.rsqrt(ms + EPS))
    return jnp.concatenate(outs, axis=-1) * g


def _gelu(x):
    c = 0.7978845608028654
    return x * (0.5 * (1.0 + jnp.tanh(c * (x + 0.044715 * (x * x * x)))))


def _silu(x):
    return x * (1.0 / (1.0 + jnp.exp(-x)))


def _log_sigmoid(x):
    return jnp.minimum(x, 0.0) - jnp.log1p(jnp.exp(-jnp.abs(x)))


def _iota(shape, dim):
    return lax.broadcasted_iota(jnp.int32, shape, dim)


def _in_proj(x, gmix, w_qkvra, w_usv, w_alpha, b_alpha):
    xn = _rms(x, gmix).astype(BF16)
    p = _dot_nt(xn, w_qkvra)
    q = p[:, 0:256] * (GLA_DK ** -0.5)
    k = p[:, 256:512]
    v = p[:, 512:1024]
    r = p[:, 1024:1536]
    xg = _dot(p[:, 1536:1664], w_alpha) + b_alpha
    logg = _log_sigmoid(xg) * (1.0 / GLA_TAU)
    p2 = _dot_nt(xn, w_usv)
    u = p2[:, 0:512]
    sv = p2[:, 512:1024]
    return q, k, v, r, u, sv, logg


def _mix_out(x, o, r, u, z, ggla, w_out):
    og = _rms_heads(o, ggla, GLA_HEADS, GLA_DV) * _silu(r)
    s_out = u * z
    y = _dot(og, w_out[0:GLA_WIDTH, :]) + _dot(s_out, w_out[GLA_WIDTH:, :])
    return x + y


def _p_mixer_kernel(x_ref, gmix_ref, wqkvra_ref, wusv_ref, walpha_ref, balpha_ref, ggla_ref, gsgu_ref,
                    ws_ref, bs_ref, wout_ref, h_ref, s_ref, st_ref):
    t = pl.program_id(1)

    @pl.when(t == 0)
    def _():
        st_ref[...] = jnp.zeros_like(st_ref)

    x = x_ref[...]
    n = TT
    subs = [slice(s * n, (s + 1) * n) for s in range(x.shape[0] // n)]
    nblk = n // GLA_BLOCK
    xn = _rms(x, gmix_ref[...]).astype(BF16)
    p = _dot_nt(xn, wqkvra_ref[...])
    q = p[:, 0:256] * (GLA_DK ** -0.5)
    k = p[:, 256:512]
    v = p[:, 512:1024]
    r = p[:, 1024:1536]
    logg = _log_sigmoid(_dot(p[:, 1536:1664], walpha_ref[...]) + balpha_ref[...]) * (1.0 / GLA_TAU)

    ri = _iota((n, n), 0)
    ci = _iota((n, n), 1)
    low = (ci <= ri).astype(BF16)
    bts = [_dot_exact_lhs(low, _split3(logg[ts])) for ts in subs]
    p2 = _dot_nt(xn, wusv_ref[...])

    headm = (ri >> 6) == (ci >> 6)
    scs = []
    for ts, bt in zip(subs, bts):
        qs_, ks_ = q[ts], k[ts]
        row = []
        for blk in range(nblk):
            r0 = blk * GLA_BLOCK
            if blk == 0:
                bq, ek = bt[0:GLA_BLOCK], jnp.exp(-bt)
            else:
                ref = bt[r0 - 1:r0, :]
                bq, ek = bt[r0:r0 + GLA_BLOCK] - ref, jnp.exp(ref - bt)
            qb = qs_[r0:r0 + GLA_BLOCK] * jnp.exp(bq)
            qstack = jnp.where(headm, jnp.concatenate([qb] * GLA_HEADS, axis=0), 0.0)
            row.append(_dot_nt(qstack, ks_ * ek))
        scs.append(row)

    u = _gelu(p2[:, 0:SGU_WIDTH])
    svn = _rms_heads(_gelu(p2[:, SGU_WIDTH:]), gsgu_ref[...], SGU_HEADS, SGU_DH)
    wm = ((ri >> 7) == (ci >> 7)) & (ci <= ri)
    rep = n // SGU_CHUNK

    def _chunk_weights(w):
        wt = jnp.concatenate([jnp.concatenate([w] * rep, axis=1)] * rep, axis=0)
        return jnp.where(wm, wt, 0.0).astype(BF16)

    ws = [_chunk_weights(ws_ref[h]) for h in range(SGU_HEADS)]
    zs = [jnp.concatenate([_dot(ws[h], svn[ts, h * SGU_DH:(h + 1) * SGU_DH]) for h in range(SGU_HEADS)], axis=1)
          + bs_ref[...] for ts in subs]

    cms = [_iota((GLA_BLOCK, n), 1) <= _iota((GLA_BLOCK, n), 0) + blk * GLA_BLOCK for blk in range(nblk)]
    o_intra = []
    for ts, row in zip(subs, scs):
        vs_ = v[ts]
        heads = []
        for h in range(GLA_HEADS):
            hs = slice(h * GLA_BLOCK, (h + 1) * GLA_BLOCK)
            sc_h = jnp.concatenate([jnp.where(cms[blk], row[blk][hs], 0.0) for blk in range(nblk)], axis=0)
            heads.append(_dot(sc_h, vs_[:, h * GLA_DV:(h + 1) * GLA_DV]))
        o_intra.append(jnp.concatenate(heads, axis=1))
    y_sgu = _dot(u * jnp.concatenate(zs, axis=0), wout_ref[GLA_WIDTH:, :])

    bdm = (_iota((GLA_WIDTH, QK_COLS), 0) >> 7) == (_iota((GLA_WIDTH, QK_COLS), 1) >> 6)
    upds = [_dot_tn(v[ts], k[ts] * jnp.exp(bt[n - 1:n, :] - bt)) for ts, bt in zip(subs, bts)]
    sts = [st_ref[...]]
    for bt, upd in zip(bts, upds):
        sts.append(jnp.exp(bt[n - 1:n, :]) * sts[-1] + jnp.where(bdm, upd, 0.0))
    st_ref[...] = sts[-1]
    o = jnp.concatenate([oi + _dot_nt(q[ts] * jnp.exp(bt), st)
                         for ts, bt, st, oi in zip(subs, bts, sts, o_intra)], axis=0)
    og = _rms_heads(o, ggla_ref[...], GLA_HEADS, GLA_DV) * _silu(r)
    h_ref[...] = x + (_dot(og, wout_ref[0:GLA_WIDTH, :]) + y_sgu)

    @pl.when(t == pl.num_programs(1) - 1)
    def _():
        s_full = st_ref[...].T
        for h in range(GLA_HEADS):
            s_ref[h] = s_full[h * GLA_DK:(h + 1) * GLA_DK, h * GLA_DV:(h + 1) * GLA_DV]


def _gla_sample_block(q, k, v, logg, sin_ref, sout_ref, row0):
    n = STOK
    ri = _iota((n, n), 0)
    ci = _iota((n, n), 1)
    same = (ri >> 2) == (ci >> 2)
    causal = same & (ci <= ri)
    parts = _split3(logg)
    b = _dot_exact_lhs(causal.astype(BF16), parts)
    bl = _dot_exact_lhs(same.astype(BF16), parts)
    qt = q * jnp.exp(b)
    kt = k * jnp.exp(-b)
    kh = k * jnp.exp(bl - b)
    headm = (_iota((GLA_HEADS * n, QK_COLS), 0) >> 7) == (_iota((GLA_HEADS * n, QK_COLS), 1) >> 6)
    qs = jnp.where(headm, jnp.concatenate([qt] * GLA_HEADS, axis=0), 0.0)
    sc = _dot_nt(qs, kt)
    tok = _iota((GLA_HEADS * n, n), 0) & (n - 1)
    col = _iota((GLA_HEADS * n, n), 1)
    sc = jnp.where(((tok >> 2) == (col >> 2)) & (col <= tok), sc, 0.0)
    ov = _dot(sc, v)
    o_intra = jnp.concatenate(
        [ov[h * n:(h + 1) * n, h * GLA_DV:(h + 1) * GLA_DV] for h in range(GLA_HEADS)], axis=1)

    nexp = SROWS * GLA_DK
    expand = ((_iota((GLA_DK, nexp), 1) & (GLA_DK - 1)) == _iota((GLA_DK, nexp), 0)).astype(BF16)
    expand_t = ((_iota((nexp, GLA_DK), 0) & (GLA_DK - 1)) == _iota((nexp, GLA_DK), 1)).astype(BF16)
    bd = (_iota((n, nexp), 0) >> 2) == (_iota((n, nexp), 1) >> 6)
    bd_t = (_iota((nexp, n), 0) >> 6) == (_iota((nexp, n), 1) >> 2)
    dec_t = jnp.exp(bl).T
    kh_t = kh.T
    outs = []
    for h in range(GLA_HEADS):
        dsl = slice(h * GLA_DK, (h + 1) * GLA_DK)
        s_in = sin_ref[row0:row0 + SROWS, h].reshape(nexp, GLA_DV)
        qe = jnp.where(bd, _dot(qt[:, dsl], expand), 0.0)
        outs.append(_dot(qe, s_in))
        ke_t = jnp.where(bd_t, _dot(expand_t, kh_t[dsl, :]), 0.0)
        upd = _dot(ke_t, v[:, h * GLA_DV:(h + 1) * GLA_DV])
        for rr in range(SROWS):
            dcol = jnp.broadcast_to(dec_t[dsl, 4 * rr:4 * rr + 1], (GLA_DK, GLA_DV))
            sout_ref[row0 + rr, h] = dcol * sin_ref[row0 + rr, h] + upd[rr * GLA_DK:(rr + 1) * GLA_DK]
    return o_intra + jnp.concatenate(outs, axis=1)


def _s_mixer_kernel(x_ref, sin_ref, gmix_ref, wqkvra_ref, wusv_ref, walpha_ref, balpha_ref, ggla_ref,
                    gsgu_ref, wsp_ref, bsp_ref, wout_ref, gx_ref, wq_ref,
                    h_ref, q_ref, sv_ref, sout_ref):
    x = x_ref[...].reshape(TT, D_MODEL)
    q, k, v, r, u, sv, logg = _in_proj(x, gmix_ref[...], wqkvra_ref[...], wusv_ref[...],
                                       walpha_ref[...], balpha_ref[...])
    u = _gelu(u)
    svn = _rms_heads(_gelu(sv), gsgu_ref[...], SGU_HEADS, SGU_DH)
    sv_ref[...] = svn.reshape(TT // 4, 4, SGU_HEADS, SGU_DH)
    z = jnp.tile(bsp_ref[...], (TT // 8, 1))
    for kk in range(4):
        shifted = svn if kk == 0 else pltpu.roll(svn, kk, 0)
        z = z + jnp.tile(wsp_ref[kk], (TT // 8, 1)) * shifted
    os_ = []
    for sb in range(TT // STOK):
        ts = slice(sb * STOK, (sb + 1) * STOK)
        os_.append(_gla_sample_block(q[ts], k[ts], v[ts], logg[ts], sin_ref, sout_ref, sb * SROWS))
    o = jnp.concatenate(os_, axis=0)
    h1 = _mix_out(x, o, r, u, z, ggla_ref[...], wout_ref)
    h_ref[...] = h1
    q_ref[...] = _dot(_rms(h1, gx_ref[...]), wq_ref[...])


def _softmax_rows(s):
    m = jnp.max(s, axis=-1, keepdims=True)
    e = jnp.exp(s - m)
    return e / jnp.sum(e, axis=-1, keepdims=True)


def _memkv_kernel(mem_ref, gmem_ref, wk_ref, wv_ref, mk_ref, mv_ref, mkb_ref, mvb_ref):
    nb = mem_ref.shape[0]
    mn = _rms(mem_ref[...].reshape(nb * N_MEM, D_MODEL), gmem_ref[...])
    mk = _dot(mn, wk_ref[...])
    mv = _dot(mn, wv_ref[...])
    for b in range(nb):
        rows = slice(b * N_MEM, (b + 1) * N_MEM)
        mk_ref[b] = mk[rows].reshape(N_MEM, MEM_HEADS, MEM_DH)
        mv_ref[b] = mv[rows].reshape(N_MEM, MEM_HEADS, MEM_DH)
        mkb_ref[b] = mk[rows].astype(BF16)
        mvb_ref[b] = mv[rows].astype(BF16)


def _attn_kernel(h_ref, gx_ref, wq_ref, wo_ref, mk_ref, mv_ref, qs_ref, ck_ref, cv_ref, o_ref, aos_ref):
    nq = MEM_HEADS * 4
    nk = N_MEM * MEM_HEADS
    own_head = (_iota((nq, nk), 1) & (MEM_HEADS - 1)) == (_iota((nq, nk), 0) >> 2)
    h1 = h_ref[...]
    q = _dot(_rms(h1, gx_ref[...]), wq_ref[...])

    scores = []
    for r in range(ATT_ROWS):
        qr = qs_ref[4 * r:4 * r + 4, :]
        qh = jnp.concatenate([qr[:, h * MEM_DH:(h + 1) * MEM_DH] for h in range(MEM_HEADS)], axis=0)
        scores.append(_dot_nt(qh, ck_ref[r].reshape(nk, MEM_DH)))

    sl = [slice(h * MEM_DH, (h + 1) * MEM_DH) for h in range(MEM_HEADS)]
    ps = [_dot_nt(q[:, sl[h]], mk_ref[:, sl[h]]) for h in range(MEM_HEADS)]

    for r in range(ATT_ROWS):
        p = _softmax_rows(jnp.where(own_head, scores[r] * (MEM_DH ** -0.5), -jnp.inf))
        o = _dot(p, cv_ref[r].reshape(nk, MEM_DH))
        for h in range(MEM_HEADS):
            aos_ref[4 * r:4 * r + 4, h * MEM_DH:(h + 1) * MEM_DH] = o[4 * h:4 * h + 4]

    outs = [_dot(_softmax_rows(ps[h] * (MEM_DH ** -0.5)), mv_ref[:, sl[h]]) for h in range(MEM_HEADS)]
    o_ref[...] = h1 + _dot(jnp.concatenate(outs, axis=1), wo_ref[...])


def _ffn_chunks(hn, wg_ref, wu_ref, cw_ref, cb_ref, wd_ref, prod_ref, shifts, sink):
    hb = hn.astype(BF16)
    for c in range(D_FF // FF_CHUNK):
        cs = slice(c * FF_CHUNK, (c + 1) * FF_CHUNK)
        g = jnp.dot(hb, wg_ref[:, cs], preferred_element_type=F32)
        s1, s2 = shifts(g, cs)
        conv = cb_ref[:, cs] + ((cw_ref[0:1, cs] * s2 + cw_ref[1:2, cs] * s1) + cw_ref[2:3, cs] * g)
        up = jnp.dot(hb, wu_ref[:, cs], preferred_element_type=F32)
        prod_ref[:, cs] = (_gelu(conv) * up).astype(BF16)
        sink(g, cs)
    return jnp.dot(prod_ref[...], wd_ref[...], preferred_element_type=F32)


def _p_ffn_kernel(h_ref, gffn_ref, wg_ref, wu_ref, cw_ref, cb_ref, wd_ref, gfin_ref,
                  y_ref, tail_ref, carry_ref, prod_ref):
    t = pl.program_id(1)

    @pl.when(t == 0)
    def _():
        carry_ref[...] = jnp.zeros_like(carry_ref)

    h2 = h_ref[...]
    m = h2.shape[0]
    row = _iota((m, FF_CHUNK), 0)

    def shifts(g, cs):
        prev = carry_ref[:, cs]
        p1 = prev[7:8, :]
        p2 = prev[6:7, :]
        s1 = jnp.where(row == 0, p1, pltpu.roll(g, 1, 0))
        s2 = jnp.where(row == 0, p2, jnp.where(row == 1, p1, pltpu.roll(g, 2, 0)))
        return s1, s2

    def sink(g, cs):
        carry_ref[:, cs] = g[m - 8:m, :]

    f = _ffn_chunks(_rms(h2, gffn_ref[...]), wg_ref, wu_ref, cw_ref, cb_ref, wd_ref, prod_ref, shifts, sink)
    y_ref[...] = _rms(h2 + f, gfin_ref[...])

    @pl.when(t == pl.num_programs(1) - 1)
    def _():
        tail_ref[...] = carry_ref[...]


def _s_ffn_kernel(h_ref, ao_ref, wo_ref, buf_ref, gffn_ref, wg_ref, wu_ref, cw_ref, cb_ref,
                  wd_ref, gfin_ref, y_ref, tail_ref, prod_ref):
    h2 = h_ref[...] + _dot(ao_ref[...], wo_ref[...])
    nrow2 = TT // 2
    tpos = _iota((TT, FF_CHUNK), 0) & 3
    tok = _iota((TT, nrow2), 0)
    ent = _iota((TT, nrow2), 1)
    same_row = (tok >> 2) == (ent >> 1)
    sel2 = (same_row & ((tok & 3) == (ent & 1))).astype(BF16)
    sel1 = (same_row & ((tok & 3) == 0) & ((ent & 1) == 1)).astype(BF16)
    sel_tail = (((_iota((nrow2, TT), 1) >> 2) == (_iota((nrow2, TT), 0) >> 1))
                & ((_iota((nrow2, TT), 1) & 3) == (_iota((nrow2, TT), 0) & 1) + 2)).astype(BF16)

    def shifts(g, cs):
        parts = _split3(buf_ref[:, :, cs].reshape(nrow2, FF_CHUNK))
        s1 = jnp.where(tpos == 0, _dot_exact_lhs(sel1, parts), pltpu.roll(g, 1, 0))
        s2 = jnp.where(tpos < 2, _dot_exact_lhs(sel2, parts), pltpu.roll(g, 2, 0))
        return s1, s2

    def sink(g, cs):
        tail_ref[:, :, cs] = _dot_exact_lhs(sel_tail, _split3(g)).reshape(TT // 4, 2, FF_CHUNK)

    f = _ffn_chunks(_rms(h2, gffn_ref[...]), wg_ref, wu_ref, cw_ref, cb_ref, wd_ref, prod_ref, shifts, sink)
    y_ref[...] = _rms(h2 + f, gfin_ref[...]).reshape(TT // 4, 4, D_MODEL)


WCAST_STEPS = 8


def _wcast_kernel(win_ref, wout_ref, wq_ref, wk_ref, wv_ref, wo_ref, wg_ref, wu_ref, wd_ref,
                  oqkvra_ref, ousv_ref, oout_ref, oq_ref, ok_ref, ov_ref, oo_ref, og_ref, ou_ref, od_ref):
    i = pl.program_id(0)
    na = 2 * QK_COLS + 2 * GLA_WIDTH + GLA_RANK
    nrow = oqkvra_ref.shape[0]
    start = pl.multiple_of(i * nrow, 16)
    blk = win_ref[pl.ds(start, nrow), :]
    oqkvra_ref[...] = jnp.where(_iota(blk.shape, 0) + i * nrow < na, blk, 0.0).astype(BF16)
    nrow2 = ousv_ref.shape[0]
    ousv_ref[...] = win_ref[pl.ds(pl.multiple_of(na + i * nrow2, 16), nrow2), :].astype(BF16)
    for src, dst in ((wout_ref, oout_ref), (wq_ref, oq_ref), (wk_ref, ok_ref), (wv_ref, ov_ref), (wo_ref, oo_ref),
                     (wg_ref, og_ref), (wu_ref, ou_ref), (wd_ref, od_ref)):
        dst[...] = src[...].astype(BF16)


def _const_spec(shape, single=True):
    nd = len(shape)
    kw = {"pipeline_mode": pl.Buffered(1)} if single else {}
    return pl.BlockSpec(shape, lambda *_: (0,) * nd, **kw)


def _params(sem):
    return pltpu.CompilerParams(dimension_semantics=sem, vmem_limit_bytes=VMEM_LIMIT)


def kernel(x_prompt, x_sample, mem_prompt, state_gla, state_conv, cache_mem_k, cache_mem_v, g_mix, w_in, w_alpha, b_alpha, g_gla_out, g_sgu, w_s, b_s, w_out, g_x, g_mem, wq_x, wk_x, wv_x, wo_x, g_ffn, w_gate, w_up, conv_w, conv_b, w_down, g_final):
    B, T, _ = x_prompt.shape
    RB, TS, _ = x_sample.shape
    assert T % TMIX == 0 and TMIX % TT == 0 and T % TBIG == 0
    assert TS == 4 and (RB * TS) % TT == 0 and RB % ATT_ROWS == 0
    ntok_s = RB * TS
    rows_per_step = TT // TS

    cut = 2 * QK_COLS + 2 * GLA_WIDTH
    d_in = w_in.shape[-1]
    rb = D_MODEL // WCAST_STEPS
    rbd = D_FF // WCAST_STEPS

    def _rows3(nrow, ncol):
        return pl.BlockSpec((None, nrow, ncol), lambda i: (0, i, 0))

    def _rows2(nrow, ncol):
        return pl.BlockSpec((nrow, ncol), lambda i: (i, 0))

    (w_qkvra, w_usv, w_out_b, wq_b, wk_b, wv_b, wo_b, wg_b, wu_b, wd_b) = pl.pallas_call(
        _wcast_kernel,
        grid=(WCAST_STEPS,),
        in_specs=[_const_spec((d_in, D_MODEL))] + [_rows3(rb, D_MODEL)] * 5 + [_rows3(rb, D_FF)] * 2
        + [_rows3(rbd, D_MODEL)],
        out_specs=[_rows2((cut + RANK_PAD) // WCAST_STEPS, D_MODEL), _rows2(2 * SGU_WIDTH // WCAST_STEPS, D_MODEL)]
        + [_rows2(rb, D_MODEL)] * 5 + [_rows2(rb, D_FF)] * 2 + [_rows2(rbd, D_MODEL)],
        out_shape=[jax.ShapeDtypeStruct((cut + RANK_PAD, D_MODEL), BF16),
                   jax.ShapeDtypeStruct((2 * SGU_WIDTH, D_MODEL), BF16)]
        + [jax.ShapeDtypeStruct((D_MODEL, D_MODEL), BF16)] * 5
        + [jax.ShapeDtypeStruct((D_MODEL, D_FF), BF16)] * 2 + [jax.ShapeDtypeStruct((D_FF, D_MODEL), BF16)],
        compiler_params=_params(("arbitrary",)),
        name="wcast",
    )(jnp.swapaxes(w_in[0], 0, 1), w_out, wq_x, wk_x, wv_x, wo_x, w_gate, w_up, w_down)

    w_alpha_p = jnp.concatenate([w_alpha[0], jnp.zeros((RANK_PAD - GLA_RANK, QK_COLS), F32)], axis=0).astype(BF16)
    b_alpha_r = b_alpha[0].reshape(1, QK_COLS)
    gmix = g_mix[0].reshape(1, D_MODEL)
    ggla = g_gla_out[0].reshape(1, GLA_WIDTH)
    gsgu = g_sgu[0].reshape(1, SGU_WIDTH)
    gx = g_x[0].reshape(1, D_MODEL)
    gmem = g_mem[0].reshape(1, D_MODEL)
    gffn = g_ffn[0].reshape(1, D_MODEL)
    gfin = g_final.reshape(1, D_MODEL)
    cw = conv_w[0]
    cb = conv_b[0].reshape(1, D_FF)
    ws_p = w_s[0]
    bs_p = jnp.tile(jnp.repeat(b_s[0].T, SGU_DH, axis=1), (TT // SGU_CHUNK, 1))
    w4 = w_s[0][:, :TS, :TS]
    ws_s = jnp.stack([
        jnp.tile(jnp.repeat(jnp.pad(jnp.diagonal(w4, offset=-kk, axis1=1, axis2=2), ((0, 0), (kk, 0))).T,
                            SGU_DH, axis=1), (2, 1))
        for kk in range(TS)])
    bs_s = jnp.tile(jnp.repeat(b_s[0][:, :TS].T, SGU_DH, axis=1), (2, 1))

    mixer_w_specs = [
        _const_spec((1, D_MODEL)), _const_spec((cut + RANK_PAD, D_MODEL)), _const_spec((2 * SGU_WIDTH, D_MODEL)),
        _const_spec((RANK_PAD, QK_COLS)),
        _const_spec((1, QK_COLS)), _const_spec((1, GLA_WIDTH)), _const_spec((1, SGU_WIDTH)),
    ]
    mixer_w = (gmix, w_qkvra, w_usv, w_alpha_p, b_alpha_r, ggla, gsgu)

    assert B % MEMKV_ROWS == 0
    kv4_spec = pl.BlockSpec((MEMKV_ROWS, N_MEM, MEM_HEADS, MEM_DH), lambda b: (b, 0, 0, 0))
    mem3_spec = pl.BlockSpec((MEMKV_ROWS, N_MEM, D_MODEL), lambda b: (b, 0, 0))
    mk_p, mv_p, mkb_p, mvb_p = pl.pallas_call(
        _memkv_kernel,
        grid=(B // MEMKV_ROWS,),
        in_specs=[mem3_spec,
                  _const_spec((1, D_MODEL)), _const_spec((D_MODEL, D_MODEL)), _const_spec((D_MODEL, D_MODEL))],
        out_specs=[kv4_spec, kv4_spec, mem3_spec, mem3_spec],
        out_shape=[jax.ShapeDtypeStruct((B, N_MEM, MEM_HEADS, MEM_DH), F32)] * 2
        + [jax.ShapeDtypeStruct((B, N_MEM, D_MODEL), BF16)] * 2,
        compiler_params=_params(("arbitrary",)),
        name="memkv",
    )(mem_prompt, gmem, wk_b, wv_b)

    tile_spec = pl.BlockSpec((None, TMIX, D_MODEL), lambda b, t: (b, t, 0))
    h1_p, s_p = pl.pallas_call(
        _p_mixer_kernel,
        grid=(B, T // TMIX),
        in_specs=[tile_spec] + mixer_w_specs + [
            _const_spec((SGU_HEADS, SGU_CHUNK, SGU_CHUNK)), _const_spec((TT, SGU_WIDTH)),
            _const_spec((D_MODEL, D_MODEL))],
        out_specs=[tile_spec,
                   pl.BlockSpec((None, GLA_HEADS, GLA_DK, GLA_DV), lambda b, t: (b, 0, 0, 0))],
        out_shape=[jax.ShapeDtypeStruct((B, T, D_MODEL), F32),
                   jax.ShapeDtypeStruct((B, GLA_HEADS, GLA_DK, GLA_DV), F32)],
        scratch_shapes=[pltpu.VMEM((GLA_WIDTH, QK_COLS), F32)],
        compiler_params=_params(("arbitrary", "arbitrary")),
        name="p_mixer",
    )(x_prompt, *mixer_w, ws_p, bs_p, w_out_b)

    s_in = state_gla[0]
    row3_spec = pl.BlockSpec((rows_per_step, TS, D_MODEL), lambda i: (i, 0, 0))
    tok_spec = pl.BlockSpec((TT, D_MODEL), lambda i: (i, 0))
    st_spec = pl.BlockSpec((rows_per_step, GLA_HEADS, GLA_DK, GLA_DV), lambda i: (i, 0, 0, 0))
    h1_s, q_s, sv_s, s_s = pl.pallas_call(
        _s_mixer_kernel,
        grid=(ntok_s // TT,),
        in_specs=[row3_spec, st_spec] + mixer_w_specs + [
            _const_spec((TS, 8, SGU_WIDTH)), _const_spec((8, SGU_WIDTH)), _const_spec((D_MODEL, D_MODEL)),
            _const_spec((1, D_MODEL)), _const_spec((D_MODEL, D_MODEL))],
        out_specs=[tok_spec, tok_spec,
                   pl.BlockSpec((rows_per_step, TS, SGU_HEADS, SGU_DH), lambda i: (i, 0, 0, 0)), st_spec],
        out_shape=[jax.ShapeDtypeStruct((ntok_s, D_MODEL), F32), jax.ShapeDtypeStruct((ntok_s, D_MODEL), F32),
                   jax.ShapeDtypeStruct((RB, TS, SGU_HEADS, SGU_DH), F32),
                   jax.ShapeDtypeStruct((RB, GLA_HEADS, GLA_DK, GLA_DV), F32)],
        compiler_params=_params(("arbitrary",)),
        name="s_mixer",
    )(x_sample, s_in, *mixer_w, ws_s, bs_s, w_out_b, gx, wq_b)

    nta = T // TATT
    assert B * nta * ATT_ROWS == RB
    att_spec = pl.BlockSpec((None, TATT, D_MODEL), lambda b, t: (b, t, 0))
    kv_spec = pl.BlockSpec((None, N_MEM, D_MODEL), lambda b, t: (b, 0, 0))
    qa_spec = pl.BlockSpec((ATT_ROWS * TS, D_MODEL), lambda b, t: (b * nta + t, 0))
    ckv_spec = pl.BlockSpec((ATT_ROWS, N_MEM, MEM_HEADS, MEM_DH), lambda b, t: (b * nta + t, 0, 0, 0))
    h2_p, ao_s = pl.pallas_call(
        _attn_kernel,
        grid=(B, nta),
        in_specs=[att_spec, _const_spec((1, D_MODEL)), _const_spec((D_MODEL, D_MODEL)),
                  _const_spec((D_MODEL, D_MODEL)), kv_spec, kv_spec, qa_spec, ckv_spec, ckv_spec],
        out_specs=[att_spec, qa_spec],
        out_shape=[jax.ShapeDtypeStruct((B, T, D_MODEL), F32), jax.ShapeDtypeStruct((ntok_s, D_MODEL), F32)],
        compiler_params=_params(("arbitrary", "arbitrary")),
        name="attn",
    )(h1_p, gx, wq_b, wo_b, mkb_p, mvb_p, q_s, cache_mem_k[0], cache_mem_v[0])

    big_spec = pl.BlockSpec((None, TBIG, D_MODEL), lambda b, t: (b, t, 0))
    ffn_w_specs = [_const_spec((1, D_MODEL)), _const_spec((D_MODEL, D_FF)), _const_spec((D_MODEL, D_FF)),
                   _const_spec((CONV_W, D_FF)), _const_spec((1, D_FF)), _const_spec((D_FF, D_MODEL)),
                   _const_spec((1, D_MODEL))]
    y_p, tail_p = pl.pallas_call(
        _p_ffn_kernel,
        grid=(B, T // TBIG),
        in_specs=[big_spec] + ffn_w_specs,
        out_specs=[big_spec, pl.BlockSpec((None, 8, D_FF), lambda b, t: (b, 0, 0))],
        out_shape=[jax.ShapeDtypeStruct((B, T, D_MODEL), F32), jax.ShapeDtypeStruct((B, 8, D_FF), F32)],
        scratch_shapes=[pltpu.VMEM((8, D_FF), F32), pltpu.VMEM((TBIG, D_FF), BF16)],
        compiler_params=_params(("arbitrary", "arbitrary")),
        name="p_ffn",
    )(h2_p, gffn, wg_b, wu_b, cw, cb, wd_b, gfin)

    hist_spec = pl.BlockSpec((rows_per_step, CONV_W - 1, D_FF), lambda i: (i, 0, 0))
    y_s, tail_s = pl.pallas_call(
        _s_ffn_kernel,
        grid=(ntok_s // TT,),
        in_specs=[tok_spec, tok_spec, _const_spec((D_MODEL, D_MODEL)), hist_spec] + ffn_w_specs,
        out_specs=[row3_spec, hist_spec],
        out_shape=[jax.ShapeDtypeStruct((RB, TS, D_MODEL), F32),
                   jax.ShapeDtypeStruct((RB, CONV_W - 1, D_FF), F32)],
        scratch_shapes=[pltpu.VMEM((TT, D_FF), BF16)],
        compiler_params=_params(("arbitrary",)),
        name="s_ffn",
    )(h1_s, ao_s, wo_b, state_conv[0], gffn, wg_b, wu_b, cw, cb, wd_b, gfin)

    return (y_p, y_s, s_p[None], tail_p[:, 6:8][None], mk_p[None], mv_p[None], s_s[None], tail_s[None], sv_s[None])
```

```python
import functools

import jax
import jax.numpy as jnp
from jax import lax
from jax.experimental import pallas as pl
from jax.experimental.pallas import tpu as pltpu

F32 = jnp.float32
BF16 = jnp.bfloat16

D_MODEL = 1024
GLA_HEADS = 4
GLA_DK = 64
GLA_DV = 128
QK_COLS = GLA_HEADS * GLA_DK
GLA_WIDTH = GLA_HEADS * GLA_DV
GLA_RANK = 16
GLA_TAU = 16.0
SGU_HEADS = 4
SGU_DH = 128
SGU_WIDTH = SGU_HEADS * SGU_DH
SGU_CHUNK = 128
N_MEM = 256
MEM_HEADS = 4
MEM_DH = 256
D_FF = 2816
CONV_W = 3
EPS = 1e-6

LANES = 128
RANK_PAD = LANES
TT = 256
TMIX = 1024
GLA_BLOCK = 64
SROWS = 32
STOK = SROWS * 4
FF_CHUNK = 256
TATT = 512
ATT_ROWS = 4
MEMKV_ROWS = 1
VMEM_LIMIT = 56 * 1024 * 1024
VMEM_LIMIT_BIG = 60 * 1024 * 1024


def _dot(a, b):
    return jnp.dot(a.astype(BF16), b.astype(BF16), preferred_element_type=F32)


def _dot_nt(a, b):
    return lax.dot_general(a.astype(BF16), b.astype(BF16), (((1,), (1,)), ((), ())),
                           preferred_element_type=F32)


def _dot_tn(a, b):
    return lax.dot_general(a.astype(BF16), b.astype(BF16), (((0,), (0,)), ((), ())),
                           preferred_element_type=F32)


def _split3(x):
    hi = x.astype(BF16)
    r1 = x - hi.astype(F32)
    mid = r1.astype(BF16)
    lo = (r1 - mid.astype(F32)).astype(BF16)
    return hi, mid, lo


def _dot_exact_lhs(sel, parts):
    hi, mid, lo = parts
    return (jnp.dot(sel, hi, preferred_element_type=F32)
            + jnp.dot(sel, mid, preferred_element_type=F32)
            + jnp.dot(sel, lo, preferred_element_type=F32))


def _rms(x, g):
    ms = jnp.mean(x * x, axis=-1, keepdims=True)
    return x * lax.rsqrt(ms + EPS) * g


def _rms_heads(x, g, nh, dh):
    outs = []
    for h in range(nh):
        xh = x[:, h * dh:(h + 1) * dh]
        ms = jnp.mean(xh * xh, axis=-1, keepdims=True)
        outs.append(xh * lax.rsqrt(ms + EPS))
    return jnp.concatenate(outs, axis=-1) * g


def _gelu(x):
    c = 0.7978845608028654
    return x * (0.5 * (1.0 + jnp.tanh(c * (x + 0.044715 * (x * x * x)))))


def _silu(x):
    return x * (1.0 / (1.0 + jnp.exp(-x)))


def _log_sigmoid(x):
    return jnp.minimum(x, 0.0) - jnp.log1p(jnp.exp(-jnp.abs(x)))


def _iota(shape, dim):
    return lax.broadcasted_iota(jnp.int32, shape, dim)


def _in_proj(x, gmix, w_qkvra, w_usv, w_alpha, b_alpha):
    xn = _rms(x, gmix).astype(BF16)
    p = _dot_nt(xn, w_qkvra)
    q = p[:, 0:256] * (GLA_DK ** -0.5)
    k = p[:, 256:512]
    v = p[:, 512:1024]
    r = p[:, 1024:1536]
    xg = _dot(p[:, 1536:1664], w_alpha) + b_alpha
    logg = _log_sigmoid(xg) * (1.0 / GLA_TAU)
    p2 = _dot_nt(xn, w_usv)
    u = p2[:, 0:512]
    sv = p2[:, 512:1024]
    return q, k, v, r, u, sv, logg


def _mix_out(x, o, r, u, z, ggla, w_out):
    og = _rms_heads(o, ggla, GLA_HEADS, GLA_DV) * _silu(r)
    s_out = u * z
    y = _dot(og, w_out[0:GLA_WIDTH, :]) + _dot(s_out, w_out[GLA_WIDTH:, :])
    return x + y


def _p_mixer_kernel(x_ref, gmix_ref, wqkvra_ref, wusv_ref, walpha_ref, balpha_ref, ggla_ref, gsgu_ref,
                    ws_ref, bs_ref, wout_ref, h_ref, s_ref, st_ref):
    t = pl.program_id(1)

    @pl.when(t == 0)
    def _():
        st_ref[...] = jnp.zeros_like(st_ref)

    x = x_ref[...]
    n = TT
    subs = [slice(s * n, (s + 1) * n) for s in range(x.shape[0] // n)]
    nblk = n // GLA_BLOCK
    xn = _rms(x, gmix_ref[...]).astype(BF16)
    p = _dot_nt(xn, wqkvra_ref[...])
    q = p[:, 0:256] * (GLA_DK ** -0.5)
    k = p[:, 256:512]
    v = p[:, 512:1024]
    r = p[:, 1024:1536]
    logg = _log_sigmoid(_dot(p[:, 1536:1664], walpha_ref[...]) + balpha_ref[...]) * (1.0 / GLA_TAU)

    ri = _iota((n, n), 0)
    ci = _iota((n, n), 1)
    low = (ci <= ri).astype(BF16)
    bts = [_dot_exact_lhs(low, _split3(logg[ts])) for ts in subs]
    p2 = _dot_nt(xn, wusv_ref[...])

    headm = (ri >> 6) == (ci >> 6)
    scs = []
    for ts, bt in zip(subs, bts):
        qs_, ks_ = q[ts], k[ts]
        row = []
        for blk in range(nblk):
            r0 = blk * GLA_BLOCK
            if blk == 0:
                bq, ek = bt[0:GLA_BLOCK], jnp.exp(-bt)
            else:
                ref = bt[r0 - 1:r0, :]
                bq, ek = bt[r0:r0 + GLA_BLOCK] - ref, jnp.exp(ref - bt)
            qb = qs_[r0:r0 + GLA_BLOCK] * jnp.exp(bq)
            qstack = jnp.where(headm, jnp.concatenate([qb] * GLA_HEADS, axis=0), 0.0)
            row.append(_dot_nt(qstack, ks_ * ek))
        scs.append(row)

    u = _gelu(p2[:, 0:SGU_WIDTH])
    svn = _rms_heads(_gelu(p2[:, SGU_WIDTH:]), gsgu_ref[...], SGU_HEADS, SGU_DH)
    wm = ((ri >> 7) == (ci >> 7)) & (ci <= ri)
    rep = n // SGU_CHUNK

    def _chunk_weights(w):
        wt = jnp.concatenate([jnp.concatenate([w] * rep, axis=1)] * rep, axis=0)
        return jnp.where(wm, wt, 0.0).astype(BF16)

    ws = [_chunk_weights(ws_ref[h]) for h in range(SGU_HEADS)]
    zs = [jnp.concatenate([_dot(ws[h], svn[ts, h * SGU_DH:(h + 1) * SGU_DH]) for h in range(SGU_HEADS)], axis=1)
          + bs_ref[...] for ts in subs]

    cms = [_iota((GLA_BLOCK, n), 1) <= _iota((GLA_BLOCK, n), 0) + blk * GLA_BLOCK for blk in range(nblk)]
    o_intra = []
    for ts, row in zip(subs, scs):
        vs_ = v[ts]
        heads = []
        for h in range(GLA_HEADS):
            hs = slice(h * GLA_BLOCK, (h + 1) * GLA_BLOCK)
            sc_h = jnp.concatenate([jnp.where(cms[blk], row[blk][hs], 0.0) for blk in range(nblk)], axis=0)
            heads.append(_dot(sc_h, vs_[:, h * GLA_DV:(h + 1) * GLA_DV]))
        o_intra.append(jnp.concatenate(heads, axis=1))
    y_sgu = _dot(u * jnp.concatenate(zs, axis=0), wout_ref[GLA_WIDTH:, :])

    bdm = (_iota((GLA_WIDTH, QK_COLS), 0) >> 7) == (_iota((GLA_WIDTH, QK_COLS), 1) >> 6)
    upds = [_dot_tn(v[ts], k[ts] * jnp.exp(bt[n - 1:n, :] - bt)) for ts, bt in zip(subs, bts)]
    sts = [st_ref[...]]
    for bt, upd in zip(bts, upds):
        sts.append(jnp.exp(bt[n - 1:n, :]) * sts[-1] + jnp.where(bdm, upd, 0.0))
    st_ref[...] = sts[-1]
    o = jnp.concatenate([oi + _dot_nt(q[ts] * jnp.exp(bt), st)
                         for ts, bt, st, oi in zip(subs, bts, sts, o_intra)], axis=0)
    og = _rms_heads(o, ggla_ref[...], GLA_HEADS, GLA_DV) * _silu(r)
    h_ref[...] = x + (_dot(og, wout_ref[0:GLA_WIDTH, :]) + y_sgu)

    @pl.when(t == pl.num_programs(1) - 1)
    def _():
        s_full = st_ref[...].T
        for h in range(GLA_HEADS):
            s_ref[h] = s_full[h * GLA_DK:(h + 1) * GLA_DK, h * GLA_DV:(h + 1) * GLA_DV]


def _gla_sample_block(q, k, v, logg, sin_ref, sout_ref, row0):
    n = STOK
    ri = _iota((n, n), 0)
    ci = _iota((n, n), 1)
    same = (ri >> 2) == (ci >> 2)
    causal = same & (ci <= ri)
    parts = _split3(logg)
    b = _dot_exact_lhs(causal.astype(BF16), parts)
    bl = _dot_exact_lhs(same.astype(BF16), parts)
    qt = q * jnp.exp(b)
    kt = k * jnp.exp(-b)
    kh = k * jnp.exp(bl - b)
    headm = (_iota((GLA_HEADS * n, QK_COLS), 0) >> 7) == (_iota((GLA_HEADS * n, QK_COLS), 1) >> 6)
    qs = jnp.where(headm, jnp.concatenate([qt] * GLA_HEADS, axis=0), 0.0)
    sc = _dot_nt(qs, kt)
    tok = _iota((GLA_HEADS * n, n), 0) & (n - 1)
    col = _iota((GLA_HEADS * n, n), 1)
    sc = jnp.where(((tok >> 2) == (col >> 2)) & (col <= tok), sc, 0.0)
    ov = _dot(sc, v)
    o_intra = jnp.concatenate(
        [ov[h * n:(h + 1) * n, h * GLA_DV:(h + 1) * GLA_DV] for h in range(GLA_HEADS)], axis=1)

    nexp = SROWS * GLA_DK
    expand = ((_iota((GLA_DK, nexp), 1) & (GLA_DK - 1)) == _iota((GLA_DK, nexp), 0)).astype(BF16)
    expand_t = ((_iota((nexp, GLA_DK), 0) & (GLA_DK - 1)) == _iota((nexp, GLA_DK), 1)).astype(BF16)
    bd = (_iota((n, nexp), 0) >> 2) == (_iota((n, nexp), 1) >> 6)
    bd_t = (_iota((nexp, n), 0) >> 6) == (_iota((nexp, n), 1) >> 2)
    dec_t = jnp.exp(bl).T
    kh_t = kh.T
    outs = []
    for h in range(GLA_HEADS):
        dsl = slice(h * GLA_DK, (h + 1) * GLA_DK)
        s_in = sin_ref[row0:row0 + SROWS, h].reshape(nexp, GLA_DV)
        qe = jnp.where(bd, _dot(qt[:, dsl], expand), 0.0)
        outs.append(_dot(qe, s_in))
        ke_t = jnp.where(bd_t, _dot(expand_t, kh_t[dsl, :]), 0.0)
        upd = _dot(ke_t, v[:, h * GLA_DV:(h + 1) * GLA_DV])
        for rr in range(SROWS):
            dcol = jnp.broadcast_to(dec_t[dsl, 4 * rr:4 * rr + 1], (GLA_DK, GLA_DV))
            sout_ref[row0 + rr, h] = dcol * sin_ref[row0 + rr, h] + upd[rr * GLA_DK:(rr + 1) * GLA_DK]
    return o_intra + jnp.concatenate(outs, axis=1)


def _s_mixer_kernel(x_ref, sin_ref, gmix_ref, wqkvra_ref, wusv_ref, walpha_ref, balpha_ref, ggla_ref,
                    gsgu_ref, wsp_ref, bsp_ref, wout_ref, gx_ref, wq_ref,
                    h_ref, q_ref, sv_ref, sout_ref):
    x = x_ref[...].reshape(TT, D_MODEL)
    q, k, v, r, u, sv, logg = _in_proj(x, gmix_ref[...], wqkvra_ref[...], wusv_ref[...],
                                       walpha_ref[...], balpha_ref[...])
    u = _gelu(u)
    svn = _rms_heads(_gelu(sv), gsgu_ref[...], SGU_HEADS, SGU_DH)
    sv_ref[...] = svn.reshape(TT // 4, 4, SGU_HEADS, SGU_DH)
    z = jnp.tile(bsp_ref[...], (TT // 8, 1))
    for kk in range(4):
        shifted = svn if kk == 0 else pltpu.roll(svn, kk, 0)
        z = z + jnp.tile(wsp_ref[kk], (TT // 8, 1)) * shifted
    os_ = []
    for sb in range(TT // STOK):
        ts = slice(sb * STOK, (sb + 1) * STOK)
        os_.append(_gla_sample_block(q[ts], k[ts], v[ts], logg[ts], sin_ref, sout_ref, sb * SROWS))
    o = jnp.concatenate(os_, axis=0)
    h1 = _mix_out(x, o, r, u, z, ggla_ref[...], wout_ref)
    h_ref[...] = h1
    q_ref[...] = _dot(_rms(h1, gx_ref[...]), wq_ref[...])


def _softmax_rows(s):
    m = jnp.max(s, axis=-1, keepdims=True)
    e = jnp.exp(s - m)
    return e / jnp.sum(e, axis=-1, keepdims=True)


def _memkv_kernel(mem_ref, gmem_ref, wk_ref, wv_ref, mk_ref, mv_ref, mkb_ref, mvb_ref):
    nb = mem_ref.shape[0]
    mn = _rms(mem_ref[...].reshape(nb * N_MEM, D_MODEL), gmem_ref[...])
    mk = _dot(mn, wk_ref[...])
    mv = _dot(mn, wv_ref[...])
    for b in range(nb):
        rows = slice(b * N_MEM, (b + 1) * N_MEM)
        mk_ref[b] = mk[rows].reshape(N_MEM, MEM_HEADS, MEM_DH)
        mv_ref[b] = mv[rows].reshape(N_MEM, MEM_HEADS, MEM_DH)
        mkb_ref[b] = mk[rows].astype(BF16)
        mvb_ref[b] = mv[rows].astype(BF16)


def _attn_kernel(h_ref, gx_ref, wq_ref, wo_ref, mk_ref, mv_ref, qs_ref, ck_ref, cv_ref, o_ref, aos_ref):
    nq = MEM_HEADS * 4
    nk = N_MEM * MEM_HEADS
    own_head = (_iota((nq, nk), 1) & (MEM_HEADS - 1)) == (_iota((nq, nk), 0) >> 2)
    h1 = h_ref[...]
    q = _dot(_rms(h1, gx_ref[...]), wq_ref[...])

    scores = []
    for r in range(ATT_ROWS):
        qr = qs_ref[4 * r:4 * r + 4, :]
        qh = jnp.concatenate([qr[:, h * MEM_DH:(h + 1) * MEM_DH] for h in range(MEM_HEADS)], axis=0)
        scores.append(_dot_nt(qh, ck_ref[r].reshape(nk, MEM_DH)))

    sl = [slice(h * MEM_DH, (h + 1) * MEM_DH) for h in range(MEM_HEADS)]
    ps = [_dot_nt(q[:, sl[h]], mk_ref[:, sl[h]]) for h in range(MEM_HEADS)]

    for r in range(ATT_ROWS):
        p = _softmax_rows(jnp.where(own_head, scores[r] * (MEM_DH ** -0.5), -jnp.inf))
        o = _dot(p, cv_ref[r].reshape(nk, MEM_DH))
        for h in range(MEM_HEADS):
            aos_ref[4 * r:4 * r + 4, h * MEM_DH:(h + 1) * MEM_DH] = o[4 * h:4 * h + 4]

    outs = [_dot(_softmax_rows(ps[h] * (MEM_DH ** -0.5)), mv_ref[:, sl[h]]) for h in range(MEM_HEADS)]
    o_ref[...] = h1 + _dot(jnp.concatenate(outs, axis=1), wo_ref[...])


def _ffn_chunks(hn, wg_ref, wu_ref, cw_ref, cb_ref, wd_ref, prod_ref, shifts, sink):
    hb = hn.astype(BF16)
    for c in range(D_FF // FF_CHUNK):
        cs = slice(c * FF_CHUNK, (c + 1) * FF_CHUNK)
        g = jnp.dot(hb, wg_ref[:, cs], preferred_element_type=F32)
        s1, s2 = shifts(g, cs)
        conv = cb_ref[:, cs] + ((cw_ref[0:1, cs] * s2 + cw_ref[1:2, cs] * s1) + cw_ref[2:3, cs] * g)
        up = jnp.dot(hb, wu_ref[:, cs], preferred_element_type=F32)
        prod_ref[:, cs] = (_gelu(conv) * up).astype(BF16)
        sink(g, cs)
    return jnp.dot(prod_ref[...], wd_ref[...], preferred_element_type=F32)


def _p_ffn_kernel(h_ref, gffn_ref, wg_ref, wu_ref, cw_ref, cb_ref, wd_ref, gfin_ref,
                  y_ref, tail_ref, carry_ref, prod_ref):
    t = pl.program_id(1)

    @pl.when(t == 0)
    def _():
        carry_ref[...] = jnp.zeros_like(carry_ref)

    h2 = h_ref[...]
    m = h2.shape[0]
    row = _iota((m, FF_CHUNK), 0)

    def shifts(g, cs):
        prev = carry_ref[:, cs]
        p1 = prev[7:8, :]
        p2 = prev[6:7, :]
        s1 = jnp.where(row == 0, p1, pltpu.roll(g, 1, 0))
        s2 = jnp.where(row == 0, p2, jnp.where(row == 1, p1, pltpu.roll(g, 2, 0)))
        return s1, s2

    def sink(g, cs):
        carry_ref[:, cs] = g[m - 8:m, :]

    f = _ffn_chunks(_rms(h2, gffn_ref[...]), wg_ref, wu_ref, cw_ref, cb_ref, wd_ref, prod_ref, shifts, sink)
    y_ref[...] = _rms(h2 + f, gfin_ref[...])

    @pl.when(t == pl.num_programs(1) - 1)
    def _():
        tail_ref[...] = carry_ref[...]


def _attn_ffn_kernel(h_ref, gx_ref, wq_ref, wo_ref, mk_ref, mv_ref, qs_ref, ck_ref, cv_ref,
                     gffn_ref, wg_ref, wu_ref, cw_ref, cb_ref, wd_ref, gfin_ref,
                     y_ref, aos_ref, tail_ref, h2_ref, carry_ref, prod_ref):
    _attn_kernel(h_ref, gx_ref, wq_ref, wo_ref, mk_ref, mv_ref, qs_ref, ck_ref, cv_ref, h2_ref, aos_ref)
    _p_ffn_kernel(h2_ref, gffn_ref, wg_ref, wu_ref, cw_ref, cb_ref, wd_ref, gfin_ref,
                  y_ref, tail_ref, carry_ref, prod_ref)


def _s_ffn_kernel(h_ref, ao_ref, wo_ref, buf_ref, gffn_ref, wg_ref, wu_ref, cw_ref, cb_ref,
                  wd_ref, gfin_ref, y_ref, tail_ref, prod_ref):
    h2 = h_ref[...] + _dot(ao_ref[...], wo_ref[...])
    nrow2 = TT // 2
    tpos = _iota((TT, FF_CHUNK), 0) & 3
    tok = _iota((TT, nrow2), 0)
    ent = _iota((TT, nrow2), 1)
    same_row = (tok >> 2) == (ent >> 1)
    sel2 = (same_row & ((tok & 3) == (ent & 1))).astype(BF16)
    sel1 = (same_row & ((tok & 3) == 0) & ((ent & 1) == 1)).astype(BF16)
    sel_tail = (((_iota((nrow2, TT), 1) >> 2) == (_iota((nrow2, TT), 0) >> 1))
                & ((_iota((nrow2, TT), 1) & 3) == (_iota((nrow2, TT), 0) & 1) + 2)).astype(BF16)

    def shifts(g, cs):
        parts = _split3(buf_ref[:, :, cs].reshape(nrow2, FF_CHUNK))
        s1 = jnp.where(tpos == 0, _dot_exact_lhs(sel1, parts), pltpu.roll(g, 1, 0))
        s2 = jnp.where(tpos < 2, _dot_exact_lhs(sel2, parts), pltpu.roll(g, 2, 0))
        return s1, s2

    def sink(g, cs):
        tail_ref[:, :, cs] = _dot_exact_lhs(sel_tail, _split3(g)).reshape(TT // 4, 2, FF_CHUNK)

    f = _ffn_chunks(_rms(h2, gffn_ref[...]), wg_ref, wu_ref, cw_ref, cb_ref, wd_ref, prod_ref, shifts, sink)
    y_ref[...] = _rms(h2 + f, gfin_ref[...]).reshape(TT // 4, 4, D_MODEL)


WCAST_STEPS = 8


def _wcast_kernel(win_ref, wout_ref, wq_ref, wk_ref, wv_ref, wo_ref, wg_ref, wu_ref, wd_ref,
                  oqkvra_ref, ousv_ref, oout_ref, oq_ref, ok_ref, ov_ref, oo_ref, og_ref, ou_ref, od_ref):
    i = pl.program_id(0)
    na = 2 * QK_COLS + 2 * GLA_WIDTH + GLA_RANK
    nrow = oqkvra_ref.shape[0]
    start = pl.multiple_of(i * nrow, 16)
    blk = win_ref[pl.ds(start, nrow), :]
    oqkvra_ref[...] = jnp.where(_iota(blk.shape, 0) + i * nrow < na, blk, 0.0).astype(BF16)
    nrow2 = ousv_ref.shape[0]
    ousv_ref[...] = win_ref[pl.ds(pl.multiple_of(na + i * nrow2, 16), nrow2), :].astype(BF16)
    for src, dst in ((wout_ref, oout_ref), (wq_ref, oq_ref), (wk_ref, ok_ref), (wv_ref, ov_ref), (wo_ref, oo_ref),
                     (wg_ref, og_ref), (wu_ref, ou_ref), (wd_ref, od_ref)):
        dst[...] = src[...].astype(BF16)


def _const_spec(shape, single=True):
    nd = len(shape)
    kw = {"pipeline_mode": pl.Buffered(1)} if single else {}
    return pl.BlockSpec(shape, lambda *_: (0,) * nd, **kw)


def _params(sem):
    return pltpu.CompilerParams(dimension_semantics=sem, vmem_limit_bytes=VMEM_LIMIT)


def kernel(x_prompt, x_sample, mem_prompt, state_gla, state_conv, cache_mem_k, cache_mem_v, g_mix, w_in, w_alpha, b_alpha, g_gla_out, g_sgu, w_s, b_s, w_out, g_x, g_mem, wq_x, wk_x, wv_x, wo_x, g_ffn, w_gate, w_up, conv_w, conv_b, w_down, g_final):
    B, T, _ = x_prompt.shape
    RB, TS, _ = x_sample.shape
    assert T % TMIX == 0 and TMIX % TT == 0 and T % TATT == 0
    assert TS == 4 and (RB * TS) % TT == 0 and RB % ATT_ROWS == 0
    ntok_s = RB * TS
    rows_per_step = TT // TS

    cut = 2 * QK_COLS + 2 * GLA_WIDTH
    d_in = w_in.shape[-1]
    rb = D_MODEL // WCAST_STEPS
    rbd = D_FF // WCAST_STEPS

    def _rows3(nrow, ncol):
        return pl.BlockSpec((None, nrow, ncol), lambda i: (0, i, 0))

    def _rows2(nrow, ncol):
        return pl.BlockSpec((nrow, ncol), lambda i: (i, 0))

    (w_qkvra, w_usv, w_out_b, wq_b, wk_b, wv_b, wo_b, wg_b, wu_b, wd_b) = pl.pallas_call(
        _wcast_kernel,
        grid=(WCAST_STEPS,),
        in_specs=[_const_spec((d_in, D_MODEL))] + [_rows3(rb, D_MODEL)] * 5 + [_rows3(rb, D_FF)] * 2
        + [_rows3(rbd, D_MODEL)],
        out_specs=[_rows2((cut + RANK_PAD) // WCAST_STEPS, D_MODEL), _rows2(2 * SGU_WIDTH // WCAST_STEPS, D_MODEL)]
        + [_rows2(rb, D_MODEL)] * 5 + [_rows2(rb, D_FF)] * 2 + [_rows2(rbd, D_MODEL)],
        out_shape=[jax.ShapeDtypeStruct((cut + RANK_PAD, D_MODEL), BF16),
                   jax.ShapeDtypeStruct((2 * SGU_WIDTH, D_MODEL), BF16)]
        + [jax.ShapeDtypeStruct((D_MODEL, D_MODEL), BF16)] * 5
        + [jax.ShapeDtypeStruct((D_MODEL, D_FF), BF16)] * 2 + [jax.ShapeDtypeStruct((D_FF, D_MODEL), BF16)],
        compiler_params=_params(("arbitrary",)),
        name="wcast",
    )(jnp.swapaxes(w_in[0], 0, 1), w_out, wq_x, wk_x, wv_x, wo_x, w_gate, w_up, w_down)

    w_alpha_p = jnp.concatenate([w_alpha[0], jnp.zeros((RANK_PAD - GLA_RANK, QK_COLS), F32)], axis=0).astype(BF16)
    b_alpha_r = b_alpha[0].reshape(1, QK_COLS)
    gmix = g_mix[0].reshape(1, D_MODEL)
    ggla = g_gla_out[0].reshape(1, GLA_WIDTH)
    gsgu = g_sgu[0].reshape(1, SGU_WIDTH)
    gx = g_x[0].reshape(1, D_MODEL)
    gmem = g_mem[0].reshape(1, D_MODEL)
    gffn = g_ffn[0].reshape(1, D_MODEL)
    gfin = g_final.reshape(1, D_MODEL)
    cw = conv_w[0]
    cb = conv_b[0].reshape(1, D_FF)
    ws_p = w_s[0]
    bs_p = jnp.tile(jnp.repeat(b_s[0].T, SGU_DH, axis=1), (TT // SGU_CHUNK, 1))
    w4 = w_s[0][:, :TS, :TS]
    ws_s = jnp.stack([
        jnp.tile(jnp.repeat(jnp.pad(jnp.diagonal(w4, offset=-kk, axis1=1, axis2=2), ((0, 0), (kk, 0))).T,
                            SGU_DH, axis=1), (2, 1))
        for kk in range(TS)])
    bs_s = jnp.tile(jnp.repeat(b_s[0][:, :TS].T, SGU_DH, axis=1), (2, 1))

    mixer_w_specs = [
        _const_spec((1, D_MODEL)), _const_spec((cut + RANK_PAD, D_MODEL)), _const_spec((2 * SGU_WIDTH, D_MODEL)),
        _const_spec((RANK_PAD, QK_COLS)),
        _const_spec((1, QK_COLS)), _const_spec((1, GLA_WIDTH)), _const_spec((1, SGU_WIDTH)),
    ]
    mixer_w = (gmix, w_qkvra, w_usv, w_alpha_p, b_alpha_r, ggla, gsgu)

    assert B % MEMKV_ROWS == 0
    kv4_spec = pl.BlockSpec((MEMKV_ROWS, N_MEM, MEM_HEADS, MEM_DH), lambda b: (b, 0, 0, 0))
    mem3_spec = pl.BlockSpec((MEMKV_ROWS, N_MEM, D_MODEL), lambda b: (b, 0, 0))
    mk_p, mv_p, mkb_p, mvb_p = pl.pallas_call(
        _memkv_kernel,
        grid=(B // MEMKV_ROWS,),
        in_specs=[mem3_spec,
                  _const_spec((1, D_MODEL)), _const_spec((D_MODEL, D_MODEL)), _const_spec((D_MODEL, D_MODEL))],
        out_specs=[kv4_spec, kv4_spec, mem3_spec, mem3_spec],
        out_shape=[jax.ShapeDtypeStruct((B, N_MEM, MEM_HEADS, MEM_DH), F32)] * 2
        + [jax.ShapeDtypeStruct((B, N_MEM, D_MODEL), BF16)] * 2,
        compiler_params=_params(("arbitrary",)),
        name="memkv",
    )(mem_prompt, gmem, wk_b, wv_b)

    tile_spec = pl.BlockSpec((None, TMIX, D_MODEL), lambda b, t: (b, t, 0))
    h1_p, s_p = pl.pallas_call(
        _p_mixer_kernel,
        grid=(B, T // TMIX),
        in_specs=[tile_spec] + mixer_w_specs + [
            _const_spec((SGU_HEADS, SGU_CHUNK, SGU_CHUNK)), _const_spec((TT, SGU_WIDTH)),
            _const_spec((D_MODEL, D_MODEL))],
        out_specs=[tile_spec,
                   pl.BlockSpec((None, GLA_HEADS, GLA_DK, GLA_DV), lambda b, t: (b, 0, 0, 0))],
        out_shape=[jax.ShapeDtypeStruct((B, T, D_MODEL), F32),
                   jax.ShapeDtypeStruct((B, GLA_HEADS, GLA_DK, GLA_DV), F32)],
        scratch_shapes=[pltpu.VMEM((GLA_WIDTH, QK_COLS), F32)],
        compiler_params=_params(("arbitrary", "arbitrary")),
        name="p_mixer",
    )(x_prompt, *mixer_w, ws_p, bs_p, w_out_b)

    s_in = state_gla[0]
    row3_spec = pl.BlockSpec((rows_per_step, TS, D_MODEL), lambda i: (i, 0, 0))
    tok_spec = pl.BlockSpec((TT, D_MODEL), lambda i: (i, 0))
    st_spec = pl.BlockSpec((rows_per_step, GLA_HEADS, GLA_DK, GLA_DV), lambda i: (i, 0, 0, 0))
    h1_s, q_s, sv_s, s_s = pl.pallas_call(
        _s_mixer_kernel,
        grid=(ntok_s // TT,),
        in_specs=[row3_spec, st_spec] + mixer_w_specs + [
            _const_spec((TS, 8, SGU_WIDTH)), _const_spec((8, SGU_WIDTH)), _const_spec((D_MODEL, D_MODEL)),
            _const_spec((1, D_MODEL)), _const_spec((D_MODEL, D_MODEL))],
        out_specs=[tok_spec, tok_spec,
                   pl.BlockSpec((rows_per_step, TS, SGU_HEADS, SGU_DH), lambda i: (i, 0, 0, 0)), st_spec],
        out_shape=[jax.ShapeDtypeStruct((ntok_s, D_MODEL), F32), jax.ShapeDtypeStruct((ntok_s, D_MODEL), F32),
                   jax.ShapeDtypeStruct((RB, TS, SGU_HEADS, SGU_DH), F32),
                   jax.ShapeDtypeStruct((RB, GLA_HEADS, GLA_DK, GLA_DV), F32)],
        compiler_params=_params(("arbitrary",)),
        name="s_mixer",
    )(x_sample, s_in, *mixer_w, ws_s, bs_s, w_out_b, gx, wq_b)

    nta = T // TATT
    assert B * nta * ATT_ROWS == RB
    att_spec = pl.BlockSpec((None, TATT, D_MODEL), lambda b, t: (b, t, 0))
    kv_spec = pl.BlockSpec((None, N_MEM, D_MODEL), lambda b, t: (b, 0, 0))
    qa_spec = pl.BlockSpec((ATT_ROWS * TS, D_MODEL), lambda b, t: (b * nta + t, 0))
    ckv_spec = pl.BlockSpec((ATT_ROWS, N_MEM, MEM_HEADS, MEM_DH), lambda b, t: (b * nta + t, 0, 0, 0))
    ffn_w_specs = [_const_spec((1, D_MODEL)), _const_spec((D_MODEL, D_FF)), _const_spec((D_MODEL, D_FF)),
                   _const_spec((CONV_W, D_FF)), _const_spec((1, D_FF)), _const_spec((D_FF, D_MODEL)),
                   _const_spec((1, D_MODEL))]
    y_p, ao_s, tail_p = pl.pallas_call(
        _attn_ffn_kernel,
        grid=(B, nta),
        in_specs=[att_spec, _const_spec((1, D_MODEL)), _const_spec((D_MODEL, D_MODEL)),
                  _const_spec((D_MODEL, D_MODEL)), kv_spec, kv_spec, qa_spec, ckv_spec, ckv_spec] + ffn_w_specs,
        out_specs=[att_spec, qa_spec, pl.BlockSpec((None, 8, D_FF), lambda b, t: (b, 0, 0))],
        out_shape=[jax.ShapeDtypeStruct((B, T, D_MODEL), F32), jax.ShapeDtypeStruct((ntok_s, D_MODEL), F32),
                   jax.ShapeDtypeStruct((B, 8, D_FF), F32)],
        scratch_shapes=[pltpu.VMEM((TATT, D_MODEL), F32), pltpu.VMEM((8, D_FF), F32),
                        pltpu.VMEM((TATT, D_FF), BF16)],
        compiler_params=pltpu.CompilerParams(dimension_semantics=("arbitrary", "arbitrary"),
                                             vmem_limit_bytes=VMEM_LIMIT_BIG),
        name="attn_ffn",
    )(h1_p, gx, wq_b, wo_b, mkb_p, mvb_p, q_s, cache_mem_k[0], cache_mem_v[0],
      gffn, wg_b, wu_b, cw, cb, wd_b, gfin)

    hist_spec = pl.BlockSpec((rows_per_step, CONV_W - 1, D_FF), lambda i: (i, 0, 0))
    y_s, tail_s = pl.pallas_call(
        _s_ffn_kernel,
        grid=(ntok_s // TT,),
        in_specs=[tok_spec, tok_spec, _const_spec((D_MODEL, D_MODEL)), hist_spec] + ffn_w_specs,
        out_specs=[row3_spec, hist_spec],
        out_shape=[jax.ShapeDtypeStruct((RB, TS, D_MODEL), F32),
                   jax.ShapeDtypeStruct((RB, CONV_W - 1, D_FF), F32)],
        scratch_shapes=[pltpu.VMEM((TT, D_FF), BF16)],
        compiler_params=_params(("arbitrary",)),
        name="s_ffn",
    )(h1_s, ao_s, wo_b, state_conv[0], gffn, wg_b, wu_b, cw, cb, wd_b, gfin)

    return (y_p, y_s, s_p[None], tail_p[:, 6:8][None], mk_p[None], mv_p[None], s_s[None], tail_s[None], sv_s[None])
```
